```python
import math
import jax, jax.numpy as jnp
from jax import lax
import numpy as np

D_MODEL = 1024
BATCH = 8
SEQ = 8192
DEPTH = 1

CHUNK = 64
MEM_LEN = 256
EPS = 1e-6

SB_HEADS = 8
SB_HEAD_DIM = 64
SB_WIDTH = SB_HEADS * SB_HEAD_DIM
SB_Q_BLOCK = 128

DN_HEADS = 4
DN_HEAD_DIM = 128
DN_WIDTH = DN_HEADS * DN_HEAD_DIM
DN_CONV = 4

XA_HEADS = 4
XA_HEAD_DIM = 64
XA_WIDTH = XA_HEADS * XA_HEAD_DIM

N_GROUPS = 4
EXPERTS_PER_GROUP = 8
N_EXPERTS = N_GROUPS * EXPERTS_PER_GROUP
TOP_K = 2
D_EXPERT = 512
MOE_BLOCK = 128

N_BRANCHES = 2
IN_SPLIT_SIZES = (3 * SB_WIDTH, 3 * DN_WIDTH, DN_HEADS, DN_HEADS, DN_WIDTH, N_BRANCHES * D_MODEL)
IN_WIDTH = 3 * SB_WIDTH + 3 * DN_WIDTH + 2 * DN_HEADS + DN_WIDTH + N_BRANCHES * D_MODEL
MIX_WIDTH = SB_WIDTH + DN_WIDTH

kernel_name = "hybrid_stickbreak_gdn_hiermoe_block"


def rmsnorm(x, g):
    xf = x.astype(jnp.float32)
    r = lax.rsqrt(jnp.mean(xf * xf, axis=-1, keepdims=True) + EPS)
    return (xf * r * g.astype(jnp.float32)).astype(x.dtype)


def l2norm(x):
    xf = x.astype(jnp.float32)
    return (xf * lax.rsqrt(jnp.sum(xf * xf, axis=-1, keepdims=True) + EPS)).astype(x.dtype)


def causal_depthwise_conv(x, w):
    k = w.shape[0]
    return lax.conv_general_dilated(
        x, w[:, None, :].astype(x.dtype), window_strides=(1,), padding=[(k - 1, 0)],
        dimension_numbers=('NWC', 'WIO', 'NWC'), feature_group_count=x.shape[-1])


def stick_breaking_attention(q, k, v):
    b, h, s, dh = q.shape
    nb = s // SB_Q_BLOCK
    scale = dh ** -0.5
    qb = jnp.moveaxis(q.reshape(b, h, nb, SB_Q_BLOCK, dh), 2, 0)
    key_pos = jnp.arange(s)

    def block(args):
        qi, i = args
        q_pos = i * SB_Q_BLOCK + jnp.arange(SB_Q_BLOCK)
        z = jnp.einsum('bhqd,bhkd->bhqk', qi, k).astype(jnp.float32) * scale
        mask = key_pos[None, :] < q_pos[:, None]
        log_beta = jax.nn.log_sigmoid(z)
        log_keep = jnp.where(mask, jax.nn.log_sigmoid(-z), 0.0)
        later = lax.cumsum(log_keep, axis=3, reverse=True) - log_keep
        a = jnp.where(mask, jnp.exp(log_beta + later), 0.0)
        return jnp.einsum('bhqk,bhkd->bhqd', a.astype(v.dtype), v)

    out = lax.map(block, (qb, jnp.arange(nb)))
    return jnp.moveaxis(out, 0, 2).reshape(b, h, s, dh)


def gated_delta_rule(q, k, v, g, beta):
    out_dtype = v.dtype
    b, h, s, dk = q.shape
    dv = v.shape[-1]
    n, c = s // CHUNK, CHUNK
    f32 = jnp.float32
    q = (q.astype(f32) * dk ** -0.5).reshape(b, h, n, c, dk)
    k = k.astype(f32).reshape(b, h, n, c, dk)
    v = v.astype(f32).reshape(b, h, n, c, dv)
    beta = beta.astype(f32).reshape(b, h, n, c, 1)
    g = lax.cumsum(g.astype(f32).reshape(b, h, n, c), axis=3)
    incl = jnp.tril(jnp.ones((c, c), dtype=bool))
    strict = jnp.tril(jnp.ones((c, c), dtype=bool), -1)
    gamma = jnp.exp(jnp.where(incl, g[..., :, None] - g[..., None, :], -jnp.inf))
    kk = jnp.einsum('bhncd,bhnjd->bhncj', k, k)
    a = jnp.where(strict, beta * kk * gamma, 0.0)
    rhs = jnp.concatenate([v * beta, k * beta * jnp.exp(g)[..., None]], axis=-1)
    sol = lax.linalg.triangular_solve(a, rhs, left_side=True, lower=True, unit_diagonal=True)
    u, w = sol[..., :dv], sol[..., dv:]
    qk = jnp.where(incl, jnp.einsum('bhncd,bhnjd->bhncj', q, k) * gamma, 0.0)
    g_last = g[..., -1]
    q_dec = q * jnp.exp(g)[..., None]
    k_dec = k * jnp.exp(g_last[..., None] - g)[..., None]

    def step(state, inp):
        q_i, k_i, u_i, w_i, qk_i, gl_i = inp
        v_new = u_i - jnp.einsum('bhck,bhkv->bhcv', w_i, state)
        o_i = jnp.einsum('bhck,bhkv->bhcv', q_i, state) + jnp.einsum('bhcj,bhjv->bhcv', qk_i, v_new)
        state = state * jnp.exp(gl_i)[..., None, None] + jnp.einsum('bhck,bhcv->bhkv', k_i, v_new)
        return state, o_i

    state0 = jnp.zeros((b, h, dk, dv), f32)
    xs = tuple(jnp.moveaxis(t, 2, 0) for t in (q_dec, k_dec, u, w, qk, g_last))
    _, o = lax.scan(step, state0, xs)
    return jnp.moveaxis(o, 0, 2).reshape(b, h, s, dv).astype(out_dtype)


def memory_cross_attention(h, mem_n, wq, wkv, wo):
    b, s, _ = h.shape
    m = mem_n.shape[1]
    q = (h @ wq).reshape(b, s, XA_HEADS, XA_HEAD_DIM)
    kv = (mem_n @ wkv).reshape(b, m, 2, XA_HEADS, XA_HEAD_DIM)
    k, v = kv[:, :, 0], kv[:, :, 1]
    scores = jnp.einsum('bshd,bmhd->bhsm', q, k).astype(jnp.float32) * XA_HEAD_DIM ** -0.5
    p = jax.nn.softmax(scores, axis=-1).astype(v.dtype)
    o = jnp.einsum('bhsm,bmhd->bshd', p, v).reshape(b, s, XA_WIDTH)
    return o @ wo


def hierarchical_route(h, w_grp, b_grp, w_exp, b_exp):
    t = h.shape[0]
    grp_logits = (h @ w_grp).astype(jnp.float32) + b_grp.astype(jnp.float32)
    p_grp = jax.nn.softmax(grp_logits, axis=-1)
    g_sel = jnp.argmax(grp_logits, axis=-1)
    p_g = jnp.take_along_axis(p_grp, g_sel[:, None], axis=-1)
    exp_logits = ((h @ w_exp).astype(jnp.float32) + b_exp.astype(jnp.float32)).reshape(t, N_GROUPS, EXPERTS_PER_GROUP)
    in_grp = jnp.take_along_axis(exp_logits, g_sel[:, None, None], axis=1)[:, 0]
    top_val, top_loc = lax.top_k(in_grp, TOP_K)
    wts = jax.nn.softmax(top_val, axis=-1) * p_g
    idx = g_sel[:, None] * EXPERTS_PER_GROUP + top_loc
    return idx, wts


def routed_experts(h, idx, wts, w_gate, w_up, w_down):
    t, d = h.shape
    n = t * TOP_K
    flat_e = idx.reshape(n)
    order = jnp.argsort(flat_e)
    e_sorted = flat_e[order]
    tok = order // TOP_K
    counts = jnp.bincount(flat_e, length=N_EXPERTS)
    padded = (counts + MOE_BLOCK - 1) // MOE_BLOCK * MOE_BLOCK
    p_end = jnp.cumsum(padded)
    p_start = p_end - padded
    start = jnp.cumsum(counts) - counts
    dest = p_start[e_sorted] + jnp.arange(n) - start[e_sorted]
    n_blocks = n // MOE_BLOCK + N_EXPERTS
    buf = jnp.zeros((n_blocks * MOE_BLOCK, d), h.dtype).at[dest].set(h[tok])
    block_start = jnp.arange(n_blocks) * MOE_BLOCK
    block_e = jnp.clip(jnp.searchsorted(p_end, block_start, side='right'), 0, N_EXPERTS - 1)

    def expert_block(args):
        xb, e = args
        return (jax.nn.silu(xb @ w_gate[e]) * (xb @ w_up[e])) @ w_down[e]

    yb = lax.map(expert_block, (buf.reshape(n_blocks, MOE_BLOCK, d), block_e))
    y_rows = yb.reshape(n_blocks * MOE_BLOCK, d)[dest]
    w_sorted = wts.reshape(n)[order].astype(h.dtype)
    return jnp.zeros((t, d), h.dtype).at[tok].add(y_rows * w_sorted[:, None])


def setup_inputs(seed: int = 0) -> dict:
    key = jax.random.key(seed)
    ks = jax.random.split(key, 24)
    L, D = DEPTH, D_MODEL

    def nrm(k, shape, scale):
        return jax.random.normal(k, shape, jnp.float32) * scale

    dt = jnp.exp(jax.random.uniform(ks[6], (L, DN_HEADS), jnp.float32, math.log(1e-3), math.log(1e-1)))
    return {
        "x": nrm(ks[0], (BATCH, SEQ, D), 1.0),
        "mem": nrm(ks[1], (BATCH, MEM_LEN, D), 1.0),
        "g_mix": 1.0 + nrm(ks[2], (L, D), 0.02),
        "w_in": nrm(ks[3], (L, D, IN_WIDTH), D ** -0.5),
        "dn_conv_w": nrm(ks[4], (L, DN_CONV, 3 * DN_WIDTH), DN_CONV ** -0.5),
        "dn_a_log": jnp.log(jax.random.uniform(ks[5], (L, DN_HEADS), jnp.float32, 1.0, 16.0)),
        "dn_dt_bias": dt + jnp.log(-jnp.expm1(-dt)),
        "dn_norm_g": 1.0 + nrm(ks[7], (L, DN_HEAD_DIM), 0.02),
        "w_out": nrm(ks[8], (L, MIX_WIDTH, D), MIX_WIDTH ** -0.5),
        "g_xattn": 1.0 + nrm(ks[9], (L, D), 0.02),
        "g_mem": 1.0 + nrm(ks[10], (L, D), 0.02),
        "xa_wq": nrm(ks[11], (L, D, XA_WIDTH), D ** -0.5),
        "xa_wkv": nrm(ks[12], (L, D, 2 * XA_WIDTH), D ** -0.5),
        "xa_wo": nrm(ks[13], (L, XA_WIDTH, D), XA_WIDTH ** -0.5),
        "g_ffn": 1.0 + nrm(ks[14], (L, D), 0.02),
        "w_grp": nrm(ks[15], (L, D, N_GROUPS), D ** -0.5),
        "b_grp": nrm(ks[16], (L, N_GROUPS), 0.01),
        "w_exp": nrm(ks[17], (L, D, N_EXPERTS), D ** -0.5),
        "b_exp": nrm(ks[18], (L, N_EXPERTS), 0.01),
        "we_gate": nrm(ks[19], (L, N_EXPERTS, D, D_EXPERT), D ** -0.5),
        "we_up": nrm(ks[20], (L, N_EXPERTS, D, D_EXPERT), D ** -0.5),
        "we_down": nrm(ks[21], (L, N_EXPERTS, D_EXPERT, D), D_EXPERT ** -0.5),
        "g_final": 1.0 + nrm(ks[22], (D,), 0.02),
    }


def reference(x, mem, g_mix, w_in, dn_conv_w, dn_a_log, dn_dt_bias, dn_norm_g, w_out,
              g_xattn, g_mem, xa_wq, xa_wkv, xa_wo, g_ffn, w_grp, b_grp, w_exp, b_exp,
              we_gate, we_up, we_down, g_final):
    b, s, d = x.shape
    offsets = np.cumsum(IN_SPLIT_SIZES)[:-1].tolist()
    for l in range(DEPTH):
        h = rmsnorm(x, g_mix[l])
        proj = h @ w_in[l]
        sb_qkv, dn_qkv, dn_b, dn_a, dn_z, gates = jnp.split(proj, offsets, axis=-1)

        sb = jnp.transpose(sb_qkv.reshape(b, s, 3, SB_HEADS, SB_HEAD_DIM), (2, 0, 3, 1, 4))
        o_sb = stick_breaking_attention(sb[0], sb[1], sb[2])
        o_sb = jnp.transpose(o_sb, (0, 2, 1, 3)).reshape(b, s, SB_WIDTH)

        dn_qkv = jax.nn.silu(causal_depthwise_conv(dn_qkv, dn_conv_w[l]))
        dn = jnp.transpose(dn_qkv.reshape(b, s, 3, DN_HEADS, DN_HEAD_DIM), (2, 0, 3, 1, 4))
        q_dn, k_dn, v_dn = l2norm(dn[0]), l2norm(dn[1]), dn[2]
        beta = jnp.transpose(jax.nn.sigmoid(dn_b.astype(jnp.float32)), (0, 2, 1))
        g_log = -jnp.exp(dn_a_log[l].astype(jnp.float32)) * jax.nn.softplus(
            dn_a.astype(jnp.float32) + dn_dt_bias[l].astype(jnp.float32))
        g_log = jnp.transpose(g_log, (0, 2, 1))
        o_dn = gated_delta_rule(q_dn, k_dn, v_dn, g_log, beta)
        o_dn = jnp.transpose(o_dn, (0, 2, 1, 3))
        z = dn_z.reshape(b, s, DN_HEADS, DN_HEAD_DIM)
        o_dn = (rmsnorm(o_dn, dn_norm_g[l]) * jax.nn.silu(z)).reshape(b, s, DN_WIDTH)

        gate = jax.nn.sigmoid(gates.reshape(b, s, N_BRANCHES, d))
        w_o = w_out[l]
        y = gate[:, :, 0] * (o_sb @ w_o[:SB_WIDTH]) + gate[:, :, 1] * (o_dn @ w_o[SB_WIDTH:])
        x = x + y

        x = x + memory_cross_attention(rmsnorm(x, g_xattn[l]), rmsnorm(mem, g_mem[l]),
                                       xa_wq[l], xa_wkv[l], xa_wo[l])

        hf = rmsnorm(x, g_ffn[l]).reshape(b * s, d)
        idx, wts = hierarchical_route(hf, w_grp[l], b_grp[l], w_exp[l], b_exp[l])
        x = x + routed_experts(hf, idx, wts, we_gate[l], we_up[l], we_down[l]).reshape(b, s, d)
    return rmsnorm(x, g_final)
```

```python
import functools

import jax
import jax.numpy as jnp
from jax import lax
from jax.experimental import pallas as pl
from jax.experimental.pallas import tpu as pltpu

F32 = jnp.float32
BF16 = jnp.bfloat16
I32 = jnp.int32

EPS = 1e-6
LANES = 128
SUBLANES = 8

SB_HEADS, SB_DH = 8, 64
SB_W = SB_HEADS * SB_DH
DN_HEADS, DN_DH = 4, 128
DN_W = DN_HEADS * DN_DH
DN_CONV = 4
CHUNK = 64
XA_HEADS, XA_DH = 4, 64
XA_W = XA_HEADS * XA_DH
N_GROUPS, EPG = 4, 8
N_EXPERTS = N_GROUPS * EPG
TOP_K = 2
D_EXPERT = 512

F32_EXP_ZERO = 104.0

VMEM_LIMIT = 56 * 1024 * 1024


def _dot(a, b):
    return jnp.dot(a, b, preferred_element_type=F32)


def _dot_nt(a, b):
    return lax.dot_general(a, b, (((1,), (1,)), ((), ())), preferred_element_type=F32)


def _dot_tn(a, b):
    return lax.dot_general(a, b, (((0,), (0,)), ((), ())), preferred_element_type=F32)


def _split2(a):
    hi = a.astype(BF16)
    lo = (a - hi.astype(F32)).astype(BF16)
    return hi, lo


def _split3(a):
    hi = a.astype(BF16)
    r = a - hi.astype(F32)
    mid = r.astype(BF16)
    lo = (r - mid.astype(F32)).astype(BF16)
    return hi, mid, lo


def _mm3(a, b):
    ah, al = _split2(a)
    bh, bl = _split2(b)
    return _dot(ah, bh) + (_dot(ah, bl) + _dot(al, bh))


def _softplus(z):
    return jnp.maximum(z, 0.0) + jnp.log1p(jnp.exp(-jnp.abs(z)))


def _sigmoid(z):
    return 1.0 / (1.0 + jnp.exp(-z))


def _silu(z):
    return z * _sigmoid(z)


def _inproj_kernel(x_ref, g_ref, wsb_ref, wdn_ref, wba_ref, wz_ref, wg_ref,
                   sb_ref, dn_ref, ba_ref, z_ref, gt_ref):
    x = x_ref[...]
    r = lax.rsqrt(jnp.mean(x * x, axis=-1, keepdims=True) + EPS)
    h = (x * r * g_ref[...]).astype(BF16)
    sb_ref[...] = _dot(h, wsb_ref[...]).astype(BF16)
    dn_ref[...] = _dot(h, wdn_ref[...])
    ba_ref[...] = _dot(h, wba_ref[...])
    z_ref[...] = _dot(h, wz_ref[...]).astype(BF16)
    gt_ref[...] = _dot(h, wg_ref[...]).astype(BF16)


def _inproj(x2d, g_mix, wsb, wdn, wba, wz, wg, tm):
    t, d = x2d.shape
    full = lambda a: pl.BlockSpec(a.shape, lambda i: (0, 0))
    row = lambda w: pl.BlockSpec((tm, w), lambda i: (i, 0))
    return pl.pallas_call(
        _inproj_kernel,
        grid=(t // tm,),
        in_specs=[row(d), full(g_mix), full(wsb), full(wdn), full(wba), full(wz), full(wg)],
        out_specs=[row(3 * SB_W), row(3 * DN_W), row(LANES), row(DN_W), row(2 * d)],
        out_shape=[
            jax.ShapeDtypeStruct((t, 3 * SB_W), BF16),
            jax.ShapeDtypeStruct((t, 3 * DN_W), F32),
            jax.ShapeDtypeStruct((t, LANES), F32),
            jax.ShapeDtypeStruct((t, DN_W), BF16),
            jax.ShapeDtypeStruct((t, 2 * d), BF16),
        ],
        compiler_params=pltpu.CompilerParams(
            dimension_semantics=("arbitrary",), vmem_limit_bytes=VMEM_LIMIT),
        name="inproj",
    )(x2d, g_mix, wsb, wdn, wba, wz, wg)


def _deltanet_kernel(dn_ref, ba_ref, z_ref, cw_ref, alog_ref, dtb_ref, ng_ref, o_ref,
                     cbuf, qkv, gc_s, beta_s, state, *, tb):
    t_idx = pl.program_id(1)
    n_chunks = tb // CHUNK

    @pl.when(t_idx == 0)
    def _():
        cbuf[0:SUBLANES, :] = jnp.zeros((SUBLANES, 3 * DN_W), F32)
        state[...] = jnp.zeros_like(state)

    cbuf[SUBLANES:SUBLANES + tb, :] = dn_ref[...]
    conv = cw_ref[0:1, :] * cbuf[SUBLANES - 3:SUBLANES - 3 + tb, :]
    for i in range(1, DN_CONV):
        conv = conv + cw_ref[i:i + 1, :] * cbuf[SUBLANES - 3 + i:SUBLANES - 3 + i + tb, :]
    cbuf[0:SUBLANES, :] = cbuf[tb:tb + SUBLANES, :]
    act = _silu(conv)
    for h in range(DN_HEADS):
        for part in range(2):
            lo = part * DN_W + h * DN_DH
            v = act[:, lo:lo + DN_DH]
            n = v * lax.rsqrt(jnp.sum(v * v, axis=-1, keepdims=True) + EPS)
            if part == 0:
                n = n * (DN_DH ** -0.5)
            qkv[:, lo:lo + DN_DH] = n
    qkv[:, 2 * DN_W:] = act[:, 2 * DN_W:]

    ba = ba_ref[...]
    beta_s[...] = _sigmoid(ba)
    g_all = -jnp.exp(alog_ref[...]) * _softplus(ba + dtb_ref[...])
    r_i = lax.broadcasted_iota(I32, (tb, tb), 0)
    c_i = lax.broadcasted_iota(I32, (tb, tb), 1)
    tri = ((r_i // CHUNK == c_i // CHUNK) & (c_i <= r_i)).astype(BF16)
    g3 = _split3(g_all)
    gc_s[...] = _dot(tri, g3[0]) + (_dot(tri, g3[1]) + _dot(tri, g3[2]))

    ci = lax.broadcasted_iota(I32, (CHUNK, CHUNK), 0)
    cj = lax.broadcasted_iota(I32, (CHUNK, CHUNK), 1)
    incl = cj <= ci
    strict = cj < ci
    eye = (ci == cj).astype(F32)
    lane = lax.broadcasted_iota(I32, (CHUNK, LANES), 1)

    def chunk_body(c, carry):
        r0 = pl.multiple_of(c * CHUNK, CHUNK)
        rows = pl.ds(r0, CHUNK)
        gcb = gc_s[rows, :]
        betab = beta_s[rows, :]
        g_last_row = gc_s[pl.ds(r0 + CHUNK - 1, 1), :]
        gcb3 = _split3(gcb)
        for h in range(DN_HEADS):
            q = qkv[rows, h * DN_DH:(h + 1) * DN_DH]
            k = qkv[rows, DN_W + h * DN_DH:DN_W + (h + 1) * DN_DH]
            v = qkv[rows, 2 * DN_W + h * DN_DH:2 * DN_W + (h + 1) * DN_DH]
            g_col = gcb[:, DN_HEADS + h:DN_HEADS + h + 1]
            b_col = betab[:, h:h + 1]
            g_last = g_last_row[:, DN_HEADS + h:DN_HEADS + h + 1]
            sel = (lane == DN_HEADS + h).astype(BF16)
            g_row = _dot_nt(sel, gcb3[0]) + (_dot_nt(sel, gcb3[1]) + _dot_nt(sel, gcb3[2]))
            gamma = jnp.exp(jnp.where(incl, g_col - g_row, -jnp.inf))
            qb, kb = q.astype(BF16), k.astype(BF16)
            kk = _dot_nt(kb, kb)
            a = jnp.where(strict, b_col * kk * gamma, 0.0)
            inv = eye - a
            apow = a
            for _ in range(5):
                apow = _mm3(apow, apow)
                inv = inv + _mm3(inv, apow)
            eg = jnp.exp(g_col)
            rhs = jnp.concatenate([v * b_col, k * (b_col * eg)], axis=1)
            sol = _mm3(inv, rhs)
            u, w = sol[:, :DN_DH], sol[:, DN_DH:]
            qk = jnp.where(incl, _dot_nt(qb, kb) * gamma, 0.0)
            q_dec = q * eg
            k_dec = k * jnp.exp(g_last - g_col)
            s_bf = state[h].astype(BF16)
            v_new = u - _dot(w.astype(BF16), s_bf)
            vn_bf = v_new.astype(BF16)
            o = _dot(q_dec.astype(BF16), s_bf) + _dot(qk.astype(BF16), vn_bf)
            state[h] = state[h] * jnp.exp(g_last) + _dot_tn(k_dec.astype(BF16), vn_bf)
            zz = z_ref[rows, h * DN_DH:(h + 1) * DN_DH].astype(F32)
            on = o * lax.rsqrt(jnp.mean(o * o, axis=-1, keepdims=True) + EPS) * ng_ref[...]
            o_ref[rows, h * DN_DH:(h + 1) * DN_DH] = (on * _silu(zz)).astype(BF16)
        return carry

    lax.fori_loop(0, n_chunks, chunk_body, 0)


def _deltanet(dn, ba, z, conv_w, alog_row, dtb_row, norm_g, b, s, tb):
    t = b * s
    nt = s // tb
    row = lambda w: pl.BlockSpec((tb, w), lambda bi, ti: (bi * nt + ti, 0))
    full = lambda a: pl.BlockSpec(a.shape, lambda bi, ti: (0, 0))
    return pl.pallas_call(
        functools.partial(_deltanet_kernel, tb=tb),
        grid=(b, nt),
        in_specs=[row(3 * DN_W), row(LANES), row(DN_W), full(conv_w), full(alog_row),
                  full(dtb_row), full(norm_g)],
        out_specs=row(DN_W),
        out_shape=jax.ShapeDtypeStruct((t, DN_W), BF16),
        scratch_shapes=[
            pltpu.VMEM((tb + 2 * SUBLANES, 3 * DN_W), F32),
            pltpu.VMEM((tb, 3 * DN_W), F32),
            pltpu.VMEM((tb, LANES), F32),
            pltpu.VMEM((tb, LANES), F32),
            pltpu.VMEM((DN_HEADS, DN_DH, DN_DH), F32),
        ],
        compiler_params=pltpu.CompilerParams(
            dimension_semantics=("arbitrary", "arbitrary"), vmem_limit_bytes=VMEM_LIMIT),
        name="deltanet",
    )(dn, ba, z, conv_w, alog_row, dtb_row, norm_g)


def _sbattn_kernel(q_ref, k_ref, v_ref, o_ref, *, tq):
    qi = pl.program_id(2)
    q = q_ref[...]
    lane = lax.broadcasted_iota(I32, (tq, LANES), 1)
    first = lane < SB_DH
    zero = jnp.zeros_like(q)
    q_heads = (jnp.where(first, q, zero), jnp.where(first, zero, q))
    ri = lax.broadcasted_iota(I32, (tq, tq), 0)
    ci = lax.broadcasted_iota(I32, (tq, tq), 1)
    causal = ci < ri
    upper = (ri > ci).astype(BF16)

    def step(kb, carry, masked):
        k0 = pl.multiple_of(kb * tq, tq)
        kblk = k_ref[pl.ds(k0, tq), :]
        vblk = v_ref[pl.ds(k0, tq), :]
        out = []
        for hh in range(2):
            r_run, acc = carry[hh]
            z = _dot_nt(q_heads[hh], kblk)
            sp = _softplus(z)
            l = jnp.where(causal, sp, 0.0) if masked else sp
            l_hi, l_lo = _split2(l)
            later = _dot(l_hi, upper) + _dot(l_lo, upper)
            a = jnp.exp((z - sp) - later - r_run)
            if masked:
                a = jnp.where(causal, a, 0.0)
            acc = acc + _dot(a.astype(BF16), vblk)
            r_run = r_run + jnp.sum(l, axis=1, keepdims=True)
            out.append((r_run, acc))
        return tuple(out)

    init = tuple((jnp.zeros((tq, 1), F32), jnp.zeros((tq, LANES), F32)) for _ in range(2))
    carry = step(qi, init, True)

    def cond(st):
        kb, c = st
        r_min = jnp.minimum(jnp.min(c[0][0]), jnp.min(c[1][0]))
        return (kb >= 0) & (r_min < F32_EXP_ZERO)

    def body(st):
        kb, c = st
        return kb - 1, step(kb, c, False)

    _, carry = lax.while_loop(cond, body, (qi - 1, carry))
    o_ref[...] = jnp.where(first, carry[0][1], carry[1][1]).astype(BF16)


def _sbattn(sb, b, s, tq):
    t = b * s
    nq = s // tq
    hp = SB_HEADS // 2
    return pl.pallas_call(
        functools.partial(_sbattn_kernel, tq=tq),
        grid=(b, hp, nq),
        in_specs=[
            pl.BlockSpec((tq, LANES), lambda bi, hi, qi: (bi * nq + qi, hi)),
            pl.BlockSpec((s, LANES), lambda bi, hi, qi: (bi, hp + hi)),
            pl.BlockSpec((s, LANES), lambda bi, hi, qi: (bi, 2 * hp + hi)),
        ],
        out_specs=pl.BlockSpec((tq, LANES), lambda bi, hi, qi: (bi * nq + qi, hi)),
        out_shape=jax.ShapeDtypeStruct((t, SB_W), BF16),
        compiler_params=pltpu.CompilerParams(
            dimension_semantics=("arbitrary", "arbitrary", "arbitrary"),
            vmem_limit_bytes=VMEM_LIMIT),
        name="sbattn",
    )(sb, sb, sb)


def _memkv_kernel(mem_ref, g_ref, wkt_ref, wv_ref, kt_ref, v_ref):
    m = mem_ref[0]
    r = lax.rsqrt(jnp.mean(m * m, axis=-1, keepdims=True) + EPS)
    mn = (m * r * g_ref[...]).astype(BF16)
    kt = _dot_nt(wkt_ref[...], mn) * (XA_DH ** -0.5)
    v = _dot(mn, wv_ref[...])
    feat_r = lax.broadcasted_iota(I32, kt.shape, 0) // XA_DH
    feat_c = lax.broadcasted_iota(I32, v.shape, 1) // XA_DH
    for h in range(XA_HEADS):
        kt_ref[0, h] = jnp.where(feat_r == h, kt, 0.0).astype(BF16)
        v_ref[0, h] = jnp.where(feat_c == h, v, 0.0).astype(BF16)


def _memkv(mem, g_mem, wkt, wv):
    b, m, d = mem.shape
    return pl.pallas_call(
        _memkv_kernel,
        grid=(b,),
        in_specs=[
            pl.BlockSpec((1, m, d), lambda i: (i, 0, 0)),
            pl.BlockSpec(g_mem.shape, lambda i: (0, 0)),
            pl.BlockSpec(wkt.shape, lambda i: (0, 0)),
            pl.BlockSpec(wv.shape, lambda i: (0, 0)),
        ],
        out_specs=[
            pl.BlockSpec((1, XA_HEADS, XA_W, m), lambda i: (i, 0, 0, 0)),
            pl.BlockSpec((1, XA_HEADS, m, XA_W), lambda i: (i, 0, 0, 0)),
        ],
        out_shape=[
            jax.ShapeDtypeStruct((b, XA_HEADS, XA_W, m), BF16),
            jax.ShapeDtypeStruct((b, XA_HEADS, m, XA_W), BF16),
        ],
        compiler_params=pltpu.CompilerParams(dimension_semantics=("arbitrary",)),
        name="memkv",
    )(mem, g_mem, wkt, wv)


def _mix_kernel(x_ref, osb_ref, odn_ref, gt_ref, wo_ref, gx_ref, wq_ref, kt_ref, v_ref,
                wxo_ref, gf_ref, wr_hi_ref, wr_lo_ref, br_ref,
                x2_ref, hf_ref, route_ref, *, tm):
    d = x_ref.shape[1]
    x = x_ref[...]
    ya = _dot(osb_ref[...], wo_ref[0:SB_W, :])
    yb = _dot(odn_ref[...], wo_ref[SB_W:, :])
    ga = _sigmoid(gt_ref[:, 0:d].astype(F32))
    gb = _sigmoid(gt_ref[:, d:].astype(F32))
    x1 = x + (ga * ya + gb * yb)

    r1 = lax.rsqrt(jnp.mean(x1 * x1, axis=-1, keepdims=True) + EPS)
    hn = (x1 * r1 * gx_ref[...]).astype(BF16)
    q = _dot(hn, wq_ref[...]).astype(BF16)
    o = jnp.zeros((tm, XA_W), F32)
    for h in range(XA_HEADS):
        sc = _dot(q, kt_ref[0, h])
        e = jnp.exp(sc - jnp.max(sc, axis=-1, keepdims=True))
        p = e / jnp.sum(e, axis=-1, keepdims=True)
        o = o + _dot(p.astype(BF16), v_ref[0, h])
    x2 = x1 + _dot(o.astype(BF16), wxo_ref[...])
    x2_ref[...] = x2

    r2 = lax.rsqrt(jnp.mean(x2 * x2, axis=-1, keepdims=True) + EPS)
    hf = x2 * r2 * gf_ref[...]
    for s8 in range(SUBLANES):
        hf_ref[pl.ds(s8, tm, stride=SUBLANES), :] = hf[:, s8 * LANES:(s8 + 1) * LANES]

    h_hi, h_lo = _split2(hf)
    logits = (_dot(h_hi, wr_hi_ref[...]) + (_dot(h_lo, wr_hi_ref[...]) + _dot(h_hi, wr_lo_ref[...]))
              + br_ref[...])
    lane = lax.broadcasted_iota(I32, (tm, LANES), 1)
    neg = -jnp.inf
    big = jnp.int32(LANES)
    gl = jnp.where(lane < N_GROUPS, logits, neg)
    gmax = jnp.max(gl, axis=-1, keepdims=True)
    g_sel = jnp.min(jnp.where(gl == gmax, lane, big), axis=-1, keepdims=True)
    p_g = 1.0 / jnp.sum(jnp.exp(gl - gmax), axis=-1, keepdims=True)
    lo = N_GROUPS + g_sel * EPG
    el = jnp.where((lane >= lo) & (lane < lo + EPG), logits, neg)
    v1 = jnp.max(el, axis=-1, keepdims=True)
    i1 = jnp.min(jnp.where(el == v1, lane, big), axis=-1, keepdims=True)
    el2 = jnp.where(lane == i1, neg, el)
    v2 = jnp.max(el2, axis=-1, keepdims=True)
    i2 = jnp.min(jnp.where(el2 == v2, lane, big), axis=-1, keepdims=True)
    e21 = jnp.exp(v2 - v1)
    w1 = p_g / (1.0 + e21)
    w2 = p_g * e21 / (1.0 + e21)
    route = jnp.where(lane == 0, (i1 - N_GROUPS).astype(F32),
            jnp.where(lane == 1, (i2 - N_GROUPS).astype(F32),
            jnp.where(lane == 2, w1, jnp.where(lane == 3, w2, 0.0))))
    route_ref[...] = route


def _mix(x2d, osb, odn, gates, wo, gx, wq, kt, v, wxo, gf, wr_hi, wr_lo, br, s, tm):
    t, d = x2d.shape
    nb = s // tm
    row = lambda w: pl.BlockSpec((tm, w), lambda i: (i, 0))
    full = lambda a: pl.BlockSpec(a.shape, lambda i: (0,) * a.ndim)
    per_b = lambda a: pl.BlockSpec((1,) + a.shape[1:], lambda i: (i // nb, 0, 0, 0))
    return pl.pallas_call(
        functools.partial(_mix_kernel, tm=tm),
        grid=(t // tm,),
        in_specs=[row(d), row(SB_W), row(DN_W), row(2 * d), full(wo), full(gx), full(wq),
                  per_b(kt), per_b(v), full(wxo), full(gf), full(wr_hi), full(wr_lo), full(br)],
        out_specs=[row(d), pl.BlockSpec((tm * SUBLANES, LANES), lambda i: (i, 0)), row(LANES)],
        out_shape=[
            jax.ShapeDtypeStruct((t, d), F32),
            jax.ShapeDtypeStruct((t * SUBLANES, LANES), F32),
            jax.ShapeDtypeStruct((t, LANES), F32),
        ],
        compiler_params=pltpu.CompilerParams(
            dimension_semantics=("arbitrary",), vmem_limit_bytes=VMEM_LIMIT),
        name="mix",
    )(x2d, osb, odn, gates, wo, gx, wq, kt, v, wxo, gf, wr_hi, wr_lo, br)


def _row_copy(src_hbm, dst, src_row, dst_row, sem):
    return pltpu.make_async_copy(
        src_hbm.at[pl.ds(src_row * SUBLANES, SUBLANES), :],
        dst.at[pl.ds(dst_row * SUBLANES, SUBLANES), :], sem)


def _experts_kernel(tile_e_ref, nused_ref, tok_hbm, hf_hbm, wgu_ref, wd_ref, y_ref,
                    tok_s, xg, sem_i, sem_g, *, tm):
    i = pl.program_id(0)

    @pl.when(i >= nused_ref[0])
    def _():
        y_ref[...] = jnp.zeros_like(y_ref)

    @pl.when(i < nused_ref[0])
    def _():
        idx_cp = pltpu.make_async_copy(tok_hbm.at[i], tok_s, sem_i)
        idx_cp.start()
        idx_cp.wait()

        def issue(r, c):
            _row_copy(hf_hbm, xg, tok_s[r], r, sem_g).start()
            return c

        lax.fori_loop(0, tm, issue, 0)

        def drain(r, c):
            _row_copy(hf_hbm, xg, 0, r, sem_g).wait()
            return c

        lax.fori_loop(0, tm, drain, 0)

        x = jnp.concatenate(
            [xg[pl.ds(s8, tm, stride=SUBLANES), :] for s8 in range(SUBLANES)], axis=1).astype(BF16)
        gu = _dot(x, wgu_ref[0])
        hmid = (_silu(gu[:, :D_EXPERT]) * gu[:, D_EXPERT:]).astype(BF16)
        y = _dot(hmid, wd_ref[0])
        for s8 in range(SUBLANES):
            y_ref[pl.ds(s8, tm, stride=SUBLANES), :] = y[:, s8 * LANES:(s8 + 1) * LANES]


def _experts(tile_e, n_used, row_tok, hf3, wgu, wd, tm):
    n_tiles = row_tok.shape[0]
    d = wd.shape[2]
    grid_spec = pltpu.PrefetchScalarGridSpec(
        num_scalar_prefetch=2,
        grid=(n_tiles,),
        in_specs=[
            pl.BlockSpec(memory_space=pl.ANY),
            pl.BlockSpec(memory_space=pl.ANY),
            pl.BlockSpec((1,) + wgu.shape[1:], lambda i, te, nu: (te[i], 0, 0)),
            pl.BlockSpec((1,) + wd.shape[1:], lambda i, te, nu: (te[i], 0, 0)),
        ],
        out_specs=pl.BlockSpec((tm * SUBLANES, LANES), lambda i, te, nu: (i, 0)),
        scratch_shapes=[
            pltpu.SMEM((tm,), I32),
            pltpu.VMEM((tm * SUBLANES, LANES), F32),
            pltpu.SemaphoreType.DMA,
            pltpu.SemaphoreType.DMA,
        ],
    )
    return pl.pallas_call(
        functools.partial(_experts_kernel, tm=tm),
        grid_spec=grid_spec,
        out_shape=jax.ShapeDtypeStruct((n_tiles * tm * SUBLANES, LANES), F32),
        compiler_params=pltpu.CompilerParams(
            dimension_semantics=("arbitrary",), vmem_limit_bytes=VMEM_LIMIT),
        name="experts",
    )(tile_e, n_used, row_tok, hf3, wgu, wd)


def _combine_kernel(dest_hbm, y_hbm, x2_ref, route_ref, gfin_ref, o_ref,
                    dest_s, yg, sem_i, sem_g, *, tm):
    i = pl.program_id(0)
    idx_cp = pltpu.make_async_copy(dest_hbm.at[i], dest_s, sem_i)
    idx_cp.start()
    idx_cp.wait()

    def issue(r, c):
        _row_copy(y_hbm, yg, dest_s[r], r, sem_g).start()
        return c

    lax.fori_loop(0, TOP_K * tm, issue, 0)

    def drain(r, c):
        _row_copy(y_hbm, yg, 0, r, sem_g).wait()
        return c

    lax.fori_loop(0, TOP_K * tm, drain, 0)

    route = route_ref[...]
    w1 = route[:, 2:3]
    w2 = route[:, 3:4]
    pieces = []
    ss = jnp.zeros((tm, 1), F32)
    for s8 in range(SUBLANES):
        y1 = yg[pl.ds(s8, tm, stride=TOP_K * SUBLANES), :]
        y2 = yg[pl.ds(SUBLANES + s8, tm, stride=TOP_K * SUBLANES), :]
        xo = x2_ref[:, s8 * LANES:(s8 + 1) * LANES] + (y1 * w1 + y2 * w2)
        ss = ss + jnp.sum(xo * xo, axis=-1, keepdims=True)
        pieces.append(xo)
    d = x2_ref.shape[1]
    r = lax.rsqrt(ss * (1.0 / d) + EPS)
    for s8 in range(SUBLANES):
        o_ref[:, s8 * LANES:(s8 + 1) * LANES] = (
            pieces[s8] * r * gfin_ref[:, s8 * LANES:(s8 + 1) * LANES])


def _combine(slot_dest, ybuf, x2, route, g_final, tm):
    t, d = x2.shape
    row = lambda w: pl.BlockSpec((tm, w), lambda i: (i, 0))
    return pl.pallas_call(
        functools.partial(_combine_kernel, tm=tm),
        grid=(t // tm,),
        in_specs=[
            pl.BlockSpec(memory_space=pl.ANY),
            pl.BlockSpec(memory_space=pl.ANY),
            row(d), row(LANES),
            pl.BlockSpec(g_final.shape, lambda i: (0, 0)),
        ],
        out_specs=row(d),
        out_shape=jax.ShapeDtypeStruct((t, d), F32),
        scratch_shapes=[
            pltpu.SMEM((TOP_K * tm,), I32),
            pltpu.VMEM((TOP_K * tm * SUBLANES, LANES), F32),
            pltpu.SemaphoreType.DMA,
            pltpu.SemaphoreType.DMA,
        ],
        compiler_params=pltpu.CompilerParams(
            dimension_semantics=("arbitrary",), vmem_limit_bytes=VMEM_LIMIT),
        name="combine",
    )(slot_dest, ybuf, x2, route, g_final)


def _plan_dispatch(idx, moe_tm):
    t = idx.shape[0]
    n = t * TOP_K
    n_tiles = n // moe_tm + N_EXPERTS
    flat_e = idx.reshape(n)
    order = jnp.argsort(flat_e).astype(I32)
    e_sorted = flat_e[order]
    counts = jnp.bincount(flat_e, length=N_EXPERTS).astype(I32)
    padded = (counts + moe_tm - 1) // moe_tm * moe_tm
    p_end = jnp.cumsum(padded)
    p_start = p_end - padded
    start = jnp.cumsum(counts) - counts
    dest = p_start[e_sorted] + jnp.arange(n, dtype=I32) - start[e_sorted]
    row_tok = jnp.zeros((n_tiles * moe_tm,), I32).at[dest].set(order // TOP_K)
    slot_dest = jnp.zeros((n,), I32).at[order].set(dest)
    tile_start = jnp.arange(n_tiles, dtype=I32) * moe_tm
    tile_e = jnp.clip(jnp.searchsorted(p_end, tile_start, side='right'), 0, N_EXPERTS - 1).astype(I32)
    n_used = (p_end[-1] // moe_tm).astype(I32).reshape(1)
    return tile_e, n_used, row_tok.reshape(n_tiles, moe_tm), slot_dest


def _lane_row(vals, offset):
    row = jnp.zeros((1, LANES), F32)
    return row.at[0, offset:offset + vals.shape[0]].set(vals.astype(F32))


def _forward(x, mem, g_mix, w_in, dn_conv_w, dn_a_log, dn_dt_bias, dn_norm_g, w_out,
             g_xattn, g_mem, xa_wq, xa_wkv, xa_wo, g_ffn, w_grp, b_grp, w_exp, b_exp,
             we_gate, we_up, we_down, g_final, *, tm_in, tb_dn, tq_sb, tm_mix, moe_tm, tm_out):
    b, s, d = x.shape
    t = b * s
    x2d = x.reshape(t, d)
    assert g_mix.shape[0] == 1, "the final RMSNorm is fused into the (single) layer's combine step"
    for l in range(1):
        w = w_in[l]
        o0 = 3 * SB_W
        o1 = o0 + 3 * DN_W
        o2 = o1 + 2 * DN_HEADS
        o3 = o2 + DN_W
        wsb = jnp.concatenate([w[:, :SB_W] * (SB_DH ** -0.5), w[:, SB_W:o0]], axis=1).astype(BF16)
        wdn = w[:, o0:o1].astype(BF16)
        wba = jnp.pad(w[:, o1:o2], ((0, 0), (0, LANES - 2 * DN_HEADS))).astype(BF16)
        wz = w[:, o2:o3].astype(BF16)
        wg = w[:, o3:].astype(BF16)
        sb, dn, ba, z, gates = _inproj(x2d, g_mix[l][None, :], wsb, wdn, wba, wz, wg, tm_in)

        o_sb = _sbattn(sb, b, s, tq_sb)
        o_dn = _deltanet(dn, ba, z, dn_conv_w[l], _lane_row(dn_a_log[l], DN_HEADS),
                         _lane_row(dn_dt_bias[l], DN_HEADS), dn_norm_g[l][None, :], b, s, tb_dn)

        wkv = xa_wkv[l]
        kt, v = _memkv(mem, g_mem[l][None, :], wkv[:, :XA_W].T.astype(BF16),
                       wkv[:, XA_W:].astype(BF16))

        w_route = jnp.pad(jnp.concatenate([w_grp[l], w_exp[l]], axis=1),
                          ((0, 0), (0, LANES - N_GROUPS - N_EXPERTS)))
        wr_hi = w_route.astype(BF16)
        wr_lo = (w_route - wr_hi.astype(F32)).astype(BF16)
        b_route = _lane_row(jnp.concatenate([b_grp[l], b_exp[l]]), 0)
        x2, hf3, route = _mix(x2d, o_sb, o_dn, gates, w_out[l].astype(BF16), g_xattn[l][None, :],
                              xa_wq[l].astype(BF16), kt, v, xa_wo[l].astype(BF16),
                              g_ffn[l][None, :], wr_hi, wr_lo, b_route, s, tm_mix)

        idx = route[:, :TOP_K].astype(I32)
        tile_e, n_used, row_tok, slot_dest = _plan_dispatch(idx, moe_tm)
        wgu = jnp.concatenate([we_gate[l], we_up[l]], axis=2).astype(BF16)
        ybuf = _experts(tile_e, n_used, row_tok, hf3, wgu, we_down[l].astype(BF16), moe_tm)
        x2d = _combine(slot_dest.reshape(t // tm_out, TOP_K * tm_out), ybuf, x2, route,
                       g_final[None, :], tm_out)
    return x2d.reshape(b, s, d)


def kernel(x, mem, g_mix, w_in, dn_conv_w, dn_a_log, dn_dt_bias, dn_norm_g, w_out, g_xattn, g_mem,
           xa_wq, xa_wkv, xa_wo, g_ffn, w_grp, b_grp, w_exp, b_exp, we_gate, we_up, we_down, g_final):
    assert g_mix.shape[0] == 1, "single-layer block"
    return _forward(x, mem, g_mix, w_in, dn_conv_w, dn_a_log, dn_dt_bias, dn_norm_g, w_out,
                    g_xattn, g_mem, xa_wq, xa_wkv, xa_wo, g_ffn, w_grp, b_grp, w_exp, b_exp,
                    we_gate, we_up, we_down, g_final,
                    tm_in=256, tb_dn=256, tq_sb=128, tm_mix=256, moe_tm=512, tm_out=256)
```

```python
import functools

import jax
import jax.numpy as jnp
from jax import lax
from jax.experimental import pallas as pl
from jax.experimental.pallas import tpu as pltpu

F32 = jnp.float32
BF16 = jnp.bfloat16
I32 = jnp.int32

EPS = 1e-6
LANES = 128
SUBLANES = 8

SB_HEADS, SB_DH = 8, 64
SB_W = SB_HEADS * SB_DH
DN_HEADS, DN_DH = 4, 128
DN_W = DN_HEADS * DN_DH
DN_CONV = 4
CHUNK = 64
XA_HEADS, XA_DH = 4, 64
XA_W = XA_HEADS * XA_DH
N_GROUPS, EPG = 4, 8
N_EXPERTS = N_GROUPS * EPG
TOP_K = 2
D_EXPERT = 512

F32_EXP_ZERO = 104.0

VMEM_LIMIT = 56 * 1024 * 1024


def _dot(a, b):
    return jnp.dot(a, b, preferred_element_type=F32)


def _dot_nt(a, b):
    return lax.dot_general(a, b, (((1,), (1,)), ((), ())), preferred_element_type=F32)


def _dot_tn(a, b):
    return lax.dot_general(a, b, (((0,), (0,)), ((), ())), preferred_element_type=F32)


def _split2(a):
    hi = a.astype(BF16)
    lo = (a - hi.astype(F32)).astype(BF16)
    return hi, lo


def _split3(a):
    hi = a.astype(BF16)
    r = a - hi.astype(F32)
    mid = r.astype(BF16)
    lo = (r - mid.astype(F32)).astype(BF16)
    return hi, mid, lo


def _mm3(a, b):
    ah, al = _split2(a)
    bh, bl = _split2(b)
    return _dot(ah, bh) + (_dot(ah, bl) + _dot(al, bh))


def _softplus(z):
    return jnp.maximum(z, 0.0) + jnp.log1p(jnp.exp(-jnp.abs(z)))


def _sigmoid(z):
    return 1.0 / (1.0 + jnp.exp(-z))


def _silu(z):
    return z * _sigmoid(z)


def _inproj_kernel(x_ref, g_ref, wsb_ref, wdn_ref, wba_ref, wz_ref, wg_ref,
                   sb_ref, dn_ref, ba_ref, z_ref, gt_ref):
    x = x_ref[...]
    r = lax.rsqrt(jnp.mean(x * x, axis=-1, keepdims=True) + EPS)
    h = (x * r * g_ref[...]).astype(BF16)
    sb_ref[...] = _dot(h, wsb_ref[...]).astype(BF16)
    dn_ref[...] = _dot(h, wdn_ref[...])
    ba_ref[...] = _dot(h, wba_ref[...])
    z_ref[...] = _dot(h, wz_ref[...]).astype(BF16)
    gt_ref[...] = _dot(h, wg_ref[...]).astype(BF16)


def _inproj(x2d, g_mix, wsb, wdn, wba, wz, wg, tm):
    t, d = x2d.shape
    full = lambda a: pl.BlockSpec(a.shape, lambda i: (0, 0))
    row = lambda w: pl.BlockSpec((tm, w), lambda i: (i, 0))
    return pl.pallas_call(
        _inproj_kernel,
        grid=(t // tm,),
        in_specs=[row(d), full(g_mix), full(wsb), full(wdn), full(wba), full(wz), full(wg)],
        out_specs=[row(3 * SB_W), row(3 * DN_W), row(LANES), row(DN_W), row(2 * d)],
        out_shape=[
            jax.ShapeDtypeStruct((t, 3 * SB_W), BF16),
            jax.ShapeDtypeStruct((t, 3 * DN_W), F32),
            jax.ShapeDtypeStruct((t, LANES), F32),
            jax.ShapeDtypeStruct((t, DN_W), BF16),
            jax.ShapeDtypeStruct((t, 2 * d), BF16),
        ],
        compiler_params=pltpu.CompilerParams(
            dimension_semantics=("arbitrary",), vmem_limit_bytes=VMEM_LIMIT),
        name="inproj",
    )(x2d, g_mix, wsb, wdn, wba, wz, wg)


def _deltanet_kernel(dn_ref, ba_ref, z_ref, cw_ref, alog_ref, dtb_ref, ng_ref, o_ref,
                     cbuf, qkv, st0, st1, st2, st3, *, tb):
    t_idx = pl.program_id(1)
    n_chunks = tb // CHUNK
    states = (st0, st1, st2, st3)

    @pl.when(t_idx == 0)
    def _():
        cbuf[0:SUBLANES, :] = jnp.zeros((SUBLANES, 3 * DN_W), F32)
        for st in states:
            st[...] = jnp.zeros_like(st)

    cbuf[SUBLANES:SUBLANES + tb, :] = dn_ref[...]
    conv = cw_ref[0:1, :] * cbuf[SUBLANES - 3:SUBLANES - 3 + tb, :]
    for i in range(1, DN_CONV):
        conv = conv + cw_ref[i:i + 1, :] * cbuf[SUBLANES - 3 + i:SUBLANES - 3 + i + tb, :]
    cbuf[0:SUBLANES, :] = cbuf[tb:tb + SUBLANES, :]
    act = _silu(conv)
    for h in range(DN_HEADS):
        for part in range(2):
            lo = part * DN_W + h * DN_DH
            v = act[:, lo:lo + DN_DH]
            n = v * lax.rsqrt(jnp.sum(v * v, axis=-1, keepdims=True) + EPS)
            if part == 0:
                n = n * (DN_DH ** -0.5)
            qkv[:, lo:lo + DN_DH] = n
    qkv[:, 2 * DN_W:] = act[:, 2 * DN_W:]

    ba = ba_ref[...]
    beta = _sigmoid(ba)
    g_all = -jnp.exp(alog_ref[...]) * _softplus(ba + dtb_ref[...])
    r_i = lax.broadcasted_iota(I32, (tb, tb), 0)
    c_i = lax.broadcasted_iota(I32, (tb, tb), 1)
    tri = ((r_i // CHUNK == c_i // CHUNK) & (c_i <= r_i)).astype(BF16)
    g3 = _split3(g_all)
    gc = _dot(tri, g3[0]) + (_dot(tri, g3[1]) + _dot(tri, g3[2]))
    gct = gc.T

    ci = lax.broadcasted_iota(I32, (CHUNK, CHUNK), 0)
    cj = lax.broadcasted_iota(I32, (CHUNK, CHUNK), 1)
    incl = cj <= ci
    strict = cj < ci
    eye = (ci == cj).astype(F32)

    units = [(c, h) for c in range(n_chunks) for h in range(DN_HEADS)]
    q_, k_, v_, gcol_, bcol_, glast_, gamma_ = {}, {}, {}, {}, {}, {}, {}
    for un in units:
        c, h = un
        r = slice(c * CHUNK, (c + 1) * CHUNK)
        q_[un] = qkv[r, h * DN_DH:(h + 1) * DN_DH]
        k_[un] = qkv[r, DN_W + h * DN_DH:DN_W + (h + 1) * DN_DH]
        v_[un] = qkv[r, 2 * DN_W + h * DN_DH:2 * DN_W + (h + 1) * DN_DH]
        gcol_[un] = gc[r, DN_HEADS + h:DN_HEADS + h + 1]
        g_row = gct[DN_HEADS + h:DN_HEADS + h + 1, r]
        bcol_[un] = beta[r, h:h + 1]
        glast_[un] = gc[(c + 1) * CHUNK - 1:(c + 1) * CHUNK, DN_HEADS + h:DN_HEADS + h + 1]
        gamma_[un] = jnp.exp(jnp.where(incl, gcol_[un] - g_row, -jnp.inf))

    qk_kk = {}
    for un in units:
        qb, kb = q_[un].astype(BF16), k_[un].astype(BF16)
        qk_kk[un] = _dot_nt(jnp.concatenate([qb, kb], axis=0), kb)
    a_ = {un: jnp.where(strict, bcol_[un] * qk_kk[un][CHUNK:] * gamma_[un], 0.0) for un in units}

    apow = {un: _mm3(a_[un], a_[un]) for un in units}
    inv = {un: (eye - a_[un]) + _mm3(eye - a_[un], apow[un]) for un in units}
    for _ in range(4):
        apow = {un: _dot(apow[un].astype(BF16), apow[un].astype(BF16)) for un in units}
        inv = {un: inv[un] + _dot(inv[un].astype(BF16), apow[un].astype(BF16)) for un in units}

    eg_ = {un: jnp.exp(gcol_[un]) for un in units}
    rhs_ = {un: jnp.concatenate([v_[un] * bcol_[un], k_[un] * (bcol_[un] * eg_[un])], axis=1)
            for un in units}
    sol_ = {un: rhs_[un] + _mm3(inv[un] - eye, rhs_[un]) for un in units}
    wq_ = {un: jnp.concatenate([sol_[un][:, DN_DH:], q_[un] * eg_[un]], axis=0).astype(BF16)
           for un in units}
    qkm_ = {un: jnp.where(incl, qk_kk[un][:CHUNK] * gamma_[un], 0.0).astype(BF16) for un in units}
    kdec_ = {un: (k_[un] * jnp.exp(glast_[un] - gcol_[un])).astype(BF16) for un in units}

    s_cur = [st[...] for st in states]
    for c in range(n_chunks):
        r = slice(c * CHUNK, (c + 1) * CHUNK)
        hs = [(c, h) for h in range(DN_HEADS)]
        ws_qs = [_dot(wq_[un], s_cur[un[1]].astype(BF16)) for un in hs]
        vn = [(sol_[un][:, :DN_DH] - ws_qs[un[1]][:CHUNK]).astype(BF16) for un in hs]
        o_ = [ws_qs[un[1]][CHUNK:] + _dot(qkm_[un], vn[un[1]]) for un in hs]
        s_cur = [s_cur[un[1]] * jnp.exp(glast_[un]) + _dot_tn(kdec_[un], vn[un[1]]) for un in hs]
        for h in range(DN_HEADS):
            o = o_[h]
            zz = z_ref[r, h * DN_DH:(h + 1) * DN_DH].astype(F32)
            on = o * lax.rsqrt(jnp.mean(o * o, axis=-1, keepdims=True) + EPS) * ng_ref[...]
            o_ref[r, h * DN_DH:(h + 1) * DN_DH] = (on * _silu(zz)).astype(BF16)
    for h in range(DN_HEADS):
        states[h][...] = s_cur[h]


def _deltanet(dn, ba, z, conv_w, alog_row, dtb_row, norm_g, b, s, tb):
    t = b * s
    nt = s // tb
    row = lambda w: pl.BlockSpec((tb, w), lambda bi, ti: (bi * nt + ti, 0))
    full = lambda a: pl.BlockSpec(a.shape, lambda bi, ti: (0, 0))
    return pl.pallas_call(
        functools.partial(_deltanet_kernel, tb=tb),
        grid=(b, nt),
        in_specs=[row(3 * DN_W), row(LANES), row(DN_W), full(conv_w), full(alog_row),
                  full(dtb_row), full(norm_g)],
        out_specs=row(DN_W),
        out_shape=jax.ShapeDtypeStruct((t, DN_W), BF16),
        scratch_shapes=[
            pltpu.VMEM((tb + SUBLANES, 3 * DN_W), F32),
            pltpu.VMEM((tb, 3 * DN_W), F32),
        ] + [pltpu.VMEM((DN_DH, DN_DH), F32) for _ in range(DN_HEADS)],
        compiler_params=pltpu.CompilerParams(
            dimension_semantics=("arbitrary", "arbitrary"), vmem_limit_bytes=VMEM_LIMIT),
        name="deltanet",
    )(dn, ba, z, conv_w, alog_row, dtb_row, norm_g)


def _sbattn_kernel(q_ref, k_ref, v_ref, o_ref, *, tq, pairs):
    qi = pl.program_id(1)
    lane = lax.broadcasted_iota(I32, (tq, LANES), 1)
    first = lane < SB_DH
    zq = jnp.zeros((tq, LANES), BF16)
    q_st = []
    for p in range(pairs):
        q = q_ref[:, p * LANES:(p + 1) * LANES]
        q_st.append(jnp.concatenate([jnp.where(first, q, zq), jnp.where(first, zq, q)], axis=0))
    ri = lax.broadcasted_iota(I32, (2 * tq, tq), 0)
    ci = lax.broadcasted_iota(I32, (2 * tq, tq), 1)
    causal = ci < jnp.where(ri >= tq, ri - tq, ri)
    ui = lax.broadcasted_iota(I32, (tq, tq), 0)
    uj = lax.broadcasted_iota(I32, (tq, tq), 1)
    upper = (ui > uj).astype(BF16)

    def step(kb, carry, masked):
        k0 = pl.multiple_of(kb * tq, tq)
        ps = range(pairs)
        z = [_dot_nt(q_st[p], k_ref[pl.ds(k0, tq), p * LANES:(p + 1) * LANES]) for p in ps]
        sp = [_softplus(z[p]) for p in ps]
        l = [jnp.where(causal, sp[p], 0.0) for p in ps] if masked else sp
        l2 = [_split2(l[p]) for p in ps]
        later = [_dot(l2[p][0], upper) + _dot(l2[p][1], upper) for p in ps]
        a_cat, v_cat = [], []
        for p in ps:
            a = jnp.exp((z[p] - sp[p]) - later[p] - carry[p][0])
            if masked:
                a = jnp.where(causal, a, 0.0)
            a = a.astype(BF16)
            a_cat.append(jnp.concatenate([a[:tq], a[tq:]], axis=1))
            vblk = v_ref[pl.ds(k0, tq), p * LANES:(p + 1) * LANES]
            v_cat.append(jnp.concatenate(
                [jnp.where(first, vblk, zq), jnp.where(first, zq, vblk)], axis=0))
        return tuple(
            (carry[p][0] + jnp.sum(l[p], axis=1, keepdims=True),
             carry[p][1] + _dot(a_cat[p], v_cat[p])) for p in ps)

    init = tuple((jnp.zeros((2 * tq, 1), F32), jnp.zeros((tq, LANES), F32)) for _ in range(pairs))
    carry = step(qi, init, True)

    def cond(st):
        kb, c = st
        r_min = jnp.min(c[0][0])
        for p in range(1, pairs):
            r_min = jnp.minimum(r_min, jnp.min(c[p][0]))
        return (kb >= 0) & (r_min < F32_EXP_ZERO)

    def body(st):
        kb, c = st
        return kb - 1, step(kb, c, False)

    _, carry = lax.while_loop(cond, body, (qi - 1, carry))
    for p in range(pairs):
        o_ref[:, p * LANES:(p + 1) * LANES] = carry[p][1].astype(BF16)


def _sbattn(sb, b, s, tq):
    t = b * s
    nq = s // tq
    pairs = SB_HEADS // 2
    return pl.pallas_call(
        functools.partial(_sbattn_kernel, tq=tq, pairs=pairs),
        grid=(b, nq),
        in_specs=[
            pl.BlockSpec((tq, SB_W), lambda bi, qi: (bi * nq + qi, 0)),
            pl.BlockSpec((s, SB_W), lambda bi, qi: (bi, 1)),
            pl.BlockSpec((s, SB_W), lambda bi, qi: (bi, 2)),
        ],
        out_specs=pl.BlockSpec((tq, SB_W), lambda bi, qi: (bi * nq + qi, 0)),
        out_shape=jax.ShapeDtypeStruct((t, SB_W), BF16),
        compiler_params=pltpu.CompilerParams(
            dimension_semantics=("arbitrary", "arbitrary"), vmem_limit_bytes=VMEM_LIMIT),
        name="sbattn",
    )(sb, sb, sb)


def _memkv_kernel(mem_ref, g_ref, wkt_ref, wv_ref, kt_ref, v_ref):
    m = mem_ref[0]
    r = lax.rsqrt(jnp.mean(m * m, axis=-1, keepdims=True) + EPS)
    mn = (m * r * g_ref[...]).astype(BF16)
    kt = _dot_nt(wkt_ref[...], mn) * (XA_DH ** -0.5)
    v = _dot(mn, wv_ref[...])
    feat_r = lax.broadcasted_iota(I32, kt.shape, 0) // XA_DH
    feat_c = lax.broadcasted_iota(I32, v.shape, 1) // XA_DH
    for h in range(XA_HEADS):
        kt_ref[0, h] = jnp.where(feat_r == h, kt, 0.0).astype(BF16)
        v_ref[0, h] = jnp.where(feat_c == h, v, 0.0).astype(BF16)


def _memkv(mem, g_mem, wkt, wv):
    b, m, d = mem.shape
    return pl.pallas_call(
        _memkv_kernel,
        grid=(b,),
        in_specs=[
            pl.BlockSpec((1, m, d), lambda i: (i, 0, 0)),
            pl.BlockSpec(g_mem.shape, lambda i: (0, 0)),
            pl.BlockSpec(wkt.shape, lambda i: (0, 0)),
            pl.BlockSpec(wv.shape, lambda i: (0, 0)),
        ],
        out_specs=[
            pl.BlockSpec((1, XA_HEADS, XA_W, m), lambda i: (i, 0, 0, 0)),
            pl.BlockSpec((1, XA_HEADS, m, XA_W), lambda i: (i, 0, 0, 0)),
        ],
        out_shape=[
            jax.ShapeDtypeStruct((b, XA_HEADS, XA_W, m), BF16),
            jax.ShapeDtypeStruct((b, XA_HEADS, m, XA_W), BF16),
        ],
        compiler_params=pltpu.CompilerParams(dimension_semantics=("arbitrary",)),
        name="memkv",
    )(mem, g_mem, wkt, wv)


def _mix_kernel(x_ref, osb_ref, odn_ref, gt_ref, wo_ref, gx_ref, wq_ref, kt_ref, v_ref,
                wxo_ref, gf_ref, wr_hi_ref, wr_lo_ref, br_ref,
                x2_ref, hf_ref, route_ref, cnt_ref, cnt_s, *, tm):
    d = x_ref.shape[1]
    x = x_ref[...]
    ya = _dot(osb_ref[...], wo_ref[0:SB_W, :])
    yb = _dot(odn_ref[...], wo_ref[SB_W:, :])
    ga = _sigmoid(gt_ref[:, 0:d].astype(F32))
    gb = _sigmoid(gt_ref[:, d:].astype(F32))
    x1 = x + (ga * ya + gb * yb)

    r1 = lax.rsqrt(jnp.mean(x1 * x1, axis=-1, keepdims=True) + EPS)
    hn = (x1 * r1 * gx_ref[...]).astype(BF16)
    q = _dot(hn, wq_ref[...]).astype(BF16)
    o = jnp.zeros((tm, XA_W), F32)
    for h in range(XA_HEADS):
        sc = _dot(q, kt_ref[0, h])
        e = jnp.exp(sc - jnp.max(sc, axis=-1, keepdims=True))
        p = e / jnp.sum(e, axis=-1, keepdims=True)
        o = o + _dot(p.astype(BF16), v_ref[0, h])
    x2 = x1 + _dot(o.astype(BF16), wxo_ref[...])
    x2_ref[...] = x2

    r2 = lax.rsqrt(jnp.mean(x2 * x2, axis=-1, keepdims=True) + EPS)
    hf = x2 * r2 * gf_ref[...]
    for s8 in range(SUBLANES):
        hf_ref[pl.ds(s8, tm, stride=SUBLANES), :] = hf[:, s8 * LANES:(s8 + 1) * LANES]

    h_hi, h_lo = _split2(hf)
    logits = (_dot(h_hi, wr_hi_ref[...]) + (_dot(h_lo, wr_hi_ref[...]) + _dot(h_hi, wr_lo_ref[...]))
              + br_ref[...])
    lane = lax.broadcasted_iota(I32, (tm, LANES), 1)
    neg = -jnp.inf
    big = jnp.int32(LANES)
    gl = jnp.where(lane < N_GROUPS, logits, neg)
    gmax = jnp.max(gl, axis=-1, keepdims=True)
    g_sel = jnp.min(jnp.where(gl == gmax, lane, big), axis=-1, keepdims=True)
    p_g = 1.0 / jnp.sum(jnp.exp(gl - gmax), axis=-1, keepdims=True)
    lo = N_GROUPS + g_sel * EPG
    el = jnp.where((lane >= lo) & (lane < lo + EPG), logits, neg)
    v1 = jnp.max(el, axis=-1, keepdims=True)
    i1 = jnp.min(jnp.where(el == v1, lane, big), axis=-1, keepdims=True)
    el2 = jnp.where(lane == i1, neg, el)
    v2 = jnp.max(el2, axis=-1, keepdims=True)
    i2 = jnp.min(jnp.where(el2 == v2, lane, big), axis=-1, keepdims=True)
    e21 = jnp.exp(v2 - v1)
    w1 = p_g / (1.0 + e21)
    w2 = p_g * e21 / (1.0 + e21)

    @pl.when(pl.program_id(0) == 0)
    def _():
        cnt_s[...] = jnp.zeros_like(cnt_s)

    id1 = i1 - N_GROUPS
    id2 = i2 - N_GROUPS
    hit1 = lane == id1
    hit2 = lane == id2
    member = (hit1 | hit2).astype(BF16)
    ti = lax.broadcasted_iota(I32, (tm, tm), 0)
    tj = lax.broadcasted_iota(I32, (tm, tm), 1)
    before = _dot((tj < ti).astype(BF16), member) + cnt_s[...]
    rank1 = jnp.sum(jnp.where(hit1, before, 0.0), axis=-1, keepdims=True)
    rank2 = jnp.sum(jnp.where(hit2, before, 0.0), axis=-1, keepdims=True)
    cnt_new = cnt_s[...] + jnp.sum(member.astype(F32), axis=0, keepdims=True)
    cnt_s[...] = cnt_new
    cnt_ref[...] = cnt_new

    route = jnp.where(lane == 0, id1.astype(F32),
            jnp.where(lane == 1, id2.astype(F32),
            jnp.where(lane == 2, w1,
            jnp.where(lane == 3, w2,
            jnp.where(lane == 4, rank1, jnp.where(lane == 5, rank2, 0.0))))))
    route_ref[...] = route


def _mix(x2d, osb, odn, gates, wo, gx, wq, kt, v, wxo, gf, wr_hi, wr_lo, br, s, tm):
    t, d = x2d.shape
    nb = s // tm
    row = lambda w: pl.BlockSpec((tm, w), lambda i: (i, 0))
    full = lambda a: pl.BlockSpec(a.shape, lambda i: (0,) * a.ndim)
    per_b = lambda a: pl.BlockSpec((1,) + a.shape[1:], lambda i: (i // nb, 0, 0, 0))
    return pl.pallas_call(
        functools.partial(_mix_kernel, tm=tm),
        grid=(t // tm,),
        in_specs=[row(d), row(SB_W), row(DN_W), row(2 * d), full(wo), full(gx), full(wq),
                  per_b(kt), per_b(v), full(wxo), full(gf), full(wr_hi), full(wr_lo), full(br)],
        out_specs=[row(d), pl.BlockSpec((tm * SUBLANES, LANES), lambda i: (i, 0)), row(LANES),
                   pl.BlockSpec((1, LANES), lambda i: (0, 0))],
        out_shape=[
            jax.ShapeDtypeStruct((t, d), F32),
            jax.ShapeDtypeStruct((t * SUBLANES, LANES), F32),
            jax.ShapeDtypeStruct((t, LANES), F32),
            jax.ShapeDtypeStruct((1, LANES), F32),
        ],
        scratch_shapes=[pltpu.VMEM((1, LANES), F32)],
        compiler_params=pltpu.CompilerParams(
            dimension_semantics=("arbitrary",), vmem_limit_bytes=VMEM_LIMIT),
        name="mix",
    )(x2d, osb, odn, gates, wo, gx, wq, kt, v, wxo, gf, wr_hi, wr_lo, br)


def _row_copy(src_hbm, dst, src_row, dst_row, sem):
    return pltpu.make_async_copy(
        src_hbm.at[pl.ds(src_row * SUBLANES, SUBLANES), :],
        dst.at[pl.ds(dst_row * SUBLANES, SUBLANES), :], sem)


def _gather_tile(i, n_valid, idx_hbm, idx_s, src_hbm, buf, sem_i, sem_g, rows):
    slot = lax.rem(i, 2)
    nxt = 1 - slot

    def idx_copy(tile, sl):
        return pltpu.make_async_copy(idx_hbm.at[tile], idx_s.at[sl], sem_i.at[sl])

    def issue_rows(sl):
        def body(r, c):
            _row_copy(src_hbm, buf.at[sl], idx_s[sl, r], r, sem_g.at[sl]).start()
            return c

        lax.fori_loop(0, rows, body, 0, unroll=8)

    @pl.when((i == 0) & (n_valid > 0))
    def _():
        idx_copy(0, 0).start()
        idx_copy(0, 0).wait()
        issue_rows(0)

        @pl.when(n_valid > 1)
        def _():
            idx_copy(1, 1).start()

    @pl.when(i + 1 < n_valid)
    def _():
        idx_copy(i + 1, nxt).wait()
        issue_rows(nxt)

        @pl.when(i + 2 < n_valid)
        def _():
            idx_copy(i + 2, slot).start()

    @pl.when(i < n_valid)
    def _():
        pltpu.make_async_copy(src_hbm.at[pl.ds(0, rows * SUBLANES), :], buf.at[slot],
                              sem_g.at[slot]).wait()

    return slot


def _experts_kernel(tile_e_ref, nused_ref, tok_hbm, hf_hbm, wgu_ref, wd_ref, y_ref,
                    tok_s, xg, sem_i, sem_g, *, tm):
    i = pl.program_id(0)
    n_used = nused_ref[0]
    slot = _gather_tile(i, n_used, tok_hbm, tok_s, hf_hbm, xg, sem_i, sem_g, tm)

    @pl.when(i >= n_used)
    def _():
        y_ref[...] = jnp.zeros_like(y_ref)

    @pl.when(i < n_used)
    def _():
        xs = xg.at[slot]
        x = jnp.concatenate(
            [xs[pl.ds(s8, tm, stride=SUBLANES), :] for s8 in range(SUBLANES)], axis=1).astype(BF16)
        gu = _dot(x, wgu_ref[0])
        hmid = (_silu(gu[:, :D_EXPERT]) * gu[:, D_EXPERT:]).astype(BF16)
        y = _dot(hmid, wd_ref[0])
        for s8 in range(SUBLANES):
            y_ref[pl.ds(s8, tm, stride=SUBLANES), :] = y[:, s8 * LANES:(s8 + 1) * LANES]


def _experts(tile_e, n_used, row_tok, hf3, wgu, wd, tm):
    n_tiles = row_tok.shape[0]
    d = wd.shape[2]
    grid_spec = pltpu.PrefetchScalarGridSpec(
        num_scalar_prefetch=2,
        grid=(n_tiles,),
        in_specs=[
            pl.BlockSpec(memory_space=pl.ANY),
            pl.BlockSpec(memory_space=pl.ANY),
            pl.BlockSpec((1,) + wgu.shape[1:], lambda i, te, nu: (te[i], 0, 0)),
            pl.BlockSpec((1,) + wd.shape[1:], lambda i, te, nu: (te[i], 0, 0)),
        ],
        out_specs=pl.BlockSpec((tm * SUBLANES, LANES), lambda i, te, nu: (i, 0)),
        scratch_shapes=[
            pltpu.SMEM((2, tm), I32),
            pltpu.VMEM((2, tm * SUBLANES, LANES), F32),
            pltpu.SemaphoreType.DMA((2,)),
            pltpu.SemaphoreType.DMA((2,)),
        ],
    )
    return pl.pallas_call(
        functools.partial(_experts_kernel, tm=tm),
        grid_spec=grid_spec,
        out_shape=jax.ShapeDtypeStruct((n_tiles * tm * SUBLANES, LANES), F32),
        compiler_params=pltpu.CompilerParams(
            dimension_semantics=("arbitrary",), vmem_limit_bytes=VMEM_LIMIT),
        name="experts",
    )(tile_e, n_used, row_tok, hf3, wgu, wd)


def _combine_kernel(dest_hbm, y_hbm, x2_ref, route_ref, gfin_ref, o_ref,
                    dest_s, yg, sem_i, sem_g, *, tm):
    i = pl.program_id(0)
    slot = _gather_tile(i, pl.num_programs(0), dest_hbm, dest_s, y_hbm, yg, sem_i, sem_g,
                        TOP_K * tm)
    ys = yg.at[slot]

    route = route_ref[...]
    w1 = route[:, 2:3]
    w2 = route[:, 3:4]
    pieces = []
    ss = jnp.zeros((tm, 1), F32)
    for s8 in range(SUBLANES):
        y1 = ys[pl.ds(s8, tm, stride=TOP_K * SUBLANES), :]
        y2 = ys[pl.ds(SUBLANES + s8, tm, stride=TOP_K * SUBLANES), :]
        xo = x2_ref[:, s8 * LANES:(s8 + 1) * LANES] + (y1 * w1 + y2 * w2)
        ss = ss + jnp.sum(xo * xo, axis=-1, keepdims=True)
        pieces.append(xo)
    d = x2_ref.shape[1]
    r = lax.rsqrt(ss * (1.0 / d) + EPS)
    for s8 in range(SUBLANES):
        o_ref[:, s8 * LANES:(s8 + 1) * LANES] = (
            pieces[s8] * r * gfin_ref[:, s8 * LANES:(s8 + 1) * LANES])


def _combine(slot_dest, ybuf, x2, route, g_final, tm):
    t, d = x2.shape
    row = lambda w: pl.BlockSpec((tm, w), lambda i: (i, 0))
    return pl.pallas_call(
        functools.partial(_combine_kernel, tm=tm),
        grid=(t // tm,),
        in_specs=[
            pl.BlockSpec(memory_space=pl.ANY),
            pl.BlockSpec(memory_space=pl.ANY),
            row(d), row(LANES),
            pl.BlockSpec(g_final.shape, lambda i: (0, 0)),
        ],
        out_specs=row(d),
        out_shape=jax.ShapeDtypeStruct((t, d), F32),
        scratch_shapes=[
            pltpu.SMEM((2, TOP_K * tm), I32),
            pltpu.VMEM((2, TOP_K * tm * SUBLANES, LANES), F32),
            pltpu.SemaphoreType.DMA((2,)),
            pltpu.SemaphoreType.DMA((2,)),
        ],
        compiler_params=pltpu.CompilerParams(
            dimension_semantics=("arbitrary",), vmem_limit_bytes=VMEM_LIMIT),
        name="combine",
    )(slot_dest, ybuf, x2, route, g_final)


def _plan_dispatch(route, counts_row, moe_tm):
    t = route.shape[0]
    n = t * TOP_K
    n_tiles = n // moe_tm + N_EXPERTS
    e = route[:, 0:TOP_K].astype(I32)
    rank = route[:, 4:4 + TOP_K].astype(I32)
    counts = counts_row[0, :N_EXPERTS].astype(I32)
    padded = (counts + moe_tm - 1) // moe_tm * moe_tm
    p_end = jnp.cumsum(padded)
    p_start = p_end - padded
    start = jnp.cumsum(counts) - counts
    slot_dest = p_start[e] + rank
    tile_start = jnp.arange(n_tiles, dtype=I32) * moe_tm
    tile_e = jnp.minimum(jnp.sum(p_end[None, :] <= tile_start[:, None], axis=1), N_EXPERTS - 1)
    tile_e = tile_e.astype(I32)
    n_used = (p_end[-1] // moe_tm).astype(I32).reshape(1)
    order = jnp.argsort(e.reshape(n), stable=True).astype(I32)
    off = jnp.arange(moe_tm, dtype=I32)[None, :] + (tile_start - p_start[tile_e])[:, None]
    valid = off < counts[tile_e][:, None]
    src = jnp.clip(start[tile_e][:, None] + off, 0, n - 1)
    row_tok = jnp.where(valid, order[src] // TOP_K, 0)
    return tile_e, n_used, row_tok, slot_dest


def _lane_row(vals, offset):
    row = jnp.zeros((1, LANES), F32)
    return row.at[0, offset:offset + vals.shape[0]].set(vals.astype(F32))


def _forward(x, mem, g_mix, w_in, dn_conv_w, dn_a_log, dn_dt_bias, dn_norm_g, w_out,
             g_xattn, g_mem, xa_wq, xa_wkv, xa_wo, g_ffn, w_grp, b_grp, w_exp, b_exp,
             we_gate, we_up, we_down, g_final, *, tm_in, tb_dn, tq_sb, tm_mix, moe_tm, tm_out):
    b, s, d = x.shape
    t = b * s
    x2d = x.reshape(t, d)
    assert g_mix.shape[0] == 1, "the final RMSNorm is fused into the (single) layer's combine step"
    for l in range(1):
        w = w_in[l]
        o0 = 3 * SB_W
        o1 = o0 + 3 * DN_W
        o2 = o1 + 2 * DN_HEADS
        o3 = o2 + DN_W
        wsb = jnp.concatenate([w[:, :SB_W] * (SB_DH ** -0.5), w[:, SB_W:o0]], axis=1).astype(BF16)
        wdn = w[:, o0:o1].astype(BF16)
        wba = jnp.pad(w[:, o1:o2], ((0, 0), (0, LANES - 2 * DN_HEADS))).astype(BF16)
        wz = w[:, o2:o3].astype(BF16)
        wg = w[:, o3:].astype(BF16)
        sb, dn, ba, z, gates = _inproj(x2d, g_mix[l][None, :], wsb, wdn, wba, wz, wg, tm_in)

        o_sb = _sbattn(sb, b, s, tq_sb)
        o_dn = _deltanet(dn, ba, z, dn_conv_w[l], _lane_row(dn_a_log[l], DN_HEADS),
                         _lane_row(dn_dt_bias[l], DN_HEADS), dn_norm_g[l][None, :], b, s, tb_dn)

        wkv = xa_wkv[l]
        kt, v = _memkv(mem, g_mem[l][None, :], wkv[:, :XA_W].T.astype(BF16),
                       wkv[:, XA_W:].astype(BF16))

        w_route = jnp.pad(jnp.concatenate([w_grp[l], w_exp[l]], axis=1),
                          ((0, 0), (0, LANES - N_GROUPS - N_EXPERTS)))
        wr_hi = w_route.astype(BF16)
        wr_lo = (w_route - wr_hi.astype(F32)).astype(BF16)
        b_route = _lane_row(jnp.concatenate([b_grp[l], b_exp[l]]), 0)
        x2, hf3, route, counts = _mix(x2d, o_sb, o_dn, gates, w_out[l].astype(BF16),
                                      g_xattn[l][None, :], xa_wq[l].astype(BF16), kt, v,
                                      xa_wo[l].astype(BF16), g_ffn[l][None, :], wr_hi, wr_lo,
                                      b_route, s, tm_mix)

        tile_e, n_used, row_tok, slot_dest = _plan_dispatch(route, counts, moe_tm)
        wgu = jnp.concatenate([we_gate[l], we_up[l]], axis=2).astype(BF16)
        ybuf = _experts(tile_e, n_used, row_tok, hf3, wgu, we_down[l].astype(BF16), moe_tm)
        x2d = _combine(slot_dest.reshape(t // tm_out, TOP_K * tm_out), ybuf, x2, route,
                       g_final[None, :], tm_out)
    return x2d.reshape(b, s, d)


def kernel(x, mem, g_mix, w_in, dn_conv_w, dn_a_log, dn_dt_bias, dn_norm_g, w_out, g_xattn, g_mem,
           xa_wq, xa_wkv, xa_wo, g_ffn, w_grp, b_grp, w_exp, b_exp, we_gate, we_up, we_down, g_final):
    assert g_mix.shape[0] == 1, "single-layer block"
    return _forward(x, mem, g_mix, w_in, dn_conv_w, dn_a_log, dn_dt_bias, dn_norm_g, w_out,
                    g_xattn, g_mem, xa_wq, xa_wkv, xa_wo, g_ffn, w_grp, b_grp, w_exp, b_exp,
                    we_gate, we_up, we_down, g_final,
                    tm_in=256, tb_dn=256, tq_sb=128, tm_mix=256, moe_tm=512, tm_out=256)
```

```python
import functools

import jax
import jax.numpy as jnp
from jax import lax
from jax.experimental import pallas as pl
from jax.experimental.pallas import tpu as pltpu

F32 = jnp.float32
BF16 = jnp.bfloat16
I32 = jnp.int32

EPS = 1e-6
LANES = 128
SUBLANES = 8

SB_HEADS, SB_DH = 8, 64
SB_W = SB_HEADS * SB_DH
DN_HEADS, DN_DH = 4, 128
DN_W = DN_HEADS * DN_DH
DN_CONV = 4
CHUNK = 64
XA_HEADS, XA_DH = 4, 64
XA_W = XA_HEADS * XA_DH
N_GROUPS, EPG = 4, 8
N_EXPERTS = N_GROUPS * EPG
TOP_K = 2
D_EXPERT = 512

F32_EXP_ZERO = 104.0

VMEM_LIMIT = 56 * 1024 * 1024


def _dot(a, b):
    return jnp.dot(a, b, preferred_element_type=F32)


def _dot_nt(a, b):
    return lax.dot_general(a, b, (((1,), (1,)), ((), ())), preferred_element_type=F32)


def _dot_tn(a, b):
    return lax.dot_general(a, b, (((0,), (0,)), ((), ())), preferred_element_type=F32)


def _split2(a):
    hi = a.astype(BF16)
    lo = (a - hi.astype(F32)).astype(BF16)
    return hi, lo


def _split3(a):
    hi = a.astype(BF16)
    r = a - hi.astype(F32)
    mid = r.astype(BF16)
    lo = (r - mid.astype(F32)).astype(BF16)
    return hi, mid, lo


def _mm3(a, b):
    ah, al = _split2(a)
    bh, bl = _split2(b)
    return _dot(ah, bh) + (_dot(ah, bl) + _dot(al, bh))


def _softplus(z):
    return jnp.maximum(z, 0.0) + jnp.log1p(jnp.exp(-jnp.abs(z)))


def _sigmoid(z):
    return 1.0 / (1.0 + jnp.exp(-z))


def _silu(z):
    return z * _sigmoid(z)


def _inproj_kernel(x_ref, g_ref, wsb_ref, wdn_ref, wba_ref, wz_ref, wg_ref,
                   sb_ref, dn_ref, ba_ref, z_ref, gt_ref):
    x = x_ref[...]
    r = lax.rsqrt(jnp.mean(x * x, axis=-1, keepdims=True) + EPS)
    h = (x * r * g_ref[...]).astype(BF16)
    sb_ref[...] = _dot(h, wsb_ref[...]).astype(BF16)
    dn_ref[...] = _dot(h, wdn_ref[...])
    ba_ref[...] = _dot(h, wba_ref[...])
    z_ref[...] = _dot(h, wz_ref[...]).astype(BF16)
    gt_ref[...] = _dot(h, wg_ref[...]).astype(BF16)


def _inproj(x2d, g_mix, wsb, wdn, wba, wz, wg, tm):
    t, d = x2d.shape
    full = lambda a: pl.BlockSpec(a.shape, lambda i: (0, 0))
    row = lambda w: pl.BlockSpec((tm, w), lambda i: (i, 0))
    return pl.pallas_call(
        _inproj_kernel,
        grid=(t // tm,),
        in_specs=[row(d), full(g_mix), full(wsb), full(wdn), full(wba), full(wz), full(wg)],
        out_specs=[row(3 * SB_W), row(3 * DN_W), row(LANES), row(DN_W), row(2 * d)],
        out_shape=[
            jax.ShapeDtypeStruct((t, 3 * SB_W), BF16),
            jax.ShapeDtypeStruct((t, 3 * DN_W), F32),
            jax.ShapeDtypeStruct((t, LANES), F32),
            jax.ShapeDtypeStruct((t, DN_W), BF16),
            jax.ShapeDtypeStruct((t, 2 * d), BF16),
        ],
        compiler_params=pltpu.CompilerParams(
            dimension_semantics=("arbitrary",), vmem_limit_bytes=VMEM_LIMIT),
        name="inproj",
    )(x2d, g_mix, wsb, wdn, wba, wz, wg)


def _deltanet_kernel(dn_ref, ba_ref, z_ref, cw_ref, alog_ref, dtb_ref, ng_ref, o_ref,
                     cbuf, qkv, st0, st1, st2, st3, *, tb):
    t_idx = pl.program_id(1)
    n_chunks = tb // CHUNK
    states = (st0, st1, st2, st3)

    @pl.when(t_idx == 0)
    def _():
        cbuf[0:SUBLANES, :] = jnp.zeros((SUBLANES, 3 * DN_W), F32)
        for st in states:
            st[...] = jnp.zeros_like(st)

    cbuf[SUBLANES:SUBLANES + tb, :] = dn_ref[...]
    conv = cw_ref[0:1, :] * cbuf[SUBLANES - 3:SUBLANES - 3 + tb, :]
    for i in range(1, DN_CONV):
        conv = conv + cw_ref[i:i + 1, :] * cbuf[SUBLANES - 3 + i:SUBLANES - 3 + i + tb, :]
    cbuf[0:SUBLANES, :] = cbuf[tb:tb + SUBLANES, :]
    act = _silu(conv)
    for h in range(DN_HEADS):
        for part in range(2):
            lo = part * DN_W + h * DN_DH
            v = act[:, lo:lo + DN_DH]
            n = v * lax.rsqrt(jnp.sum(v * v, axis=-1, keepdims=True) + EPS)
            if part == 0:
                n = n * (DN_DH ** -0.5)
            qkv[:, lo:lo + DN_DH] = n
    qkv[:, 2 * DN_W:] = act[:, 2 * DN_W:]

    ba = ba_ref[...]
    beta = _sigmoid(ba)
    g_all = -jnp.exp(alog_ref[...]) * _softplus(ba + dtb_ref[...])
    r_i = lax.broadcasted_iota(I32, (tb, tb), 0)
    c_i = lax.broadcasted_iota(I32, (tb, tb), 1)
    tri = ((r_i // CHUNK == c_i // CHUNK) & (c_i <= r_i)).astype(BF16)
    g3 = _split3(g_all)
    gc = _dot(tri, g3[0]) + (_dot(tri, g3[1]) + _dot(tri, g3[2]))
    gct = gc.T

    ci = lax.broadcasted_iota(I32, (CHUNK, CHUNK), 0)
    cj = lax.broadcasted_iota(I32, (CHUNK, CHUNK), 1)
    incl = cj <= ci
    strict = cj < ci
    eye = (ci == cj).astype(F32)

    units = [(c, h) for c in range(n_chunks) for h in range(DN_HEADS)]
    q_, k_, v_, gcol_, bcol_, glast_, gamma_ = {}, {}, {}, {}, {}, {}, {}
    for un in units:
        c, h = un
        r = slice(c * CHUNK, (c + 1) * CHUNK)
        q_[un] = qkv[r, h * DN_DH:(h + 1) * DN_DH]
        k_[un] = qkv[r, DN_W + h * DN_DH:DN_W + (h + 1) * DN_DH]
        v_[un] = qkv[r, 2 * DN_W + h * DN_DH:2 * DN_W + (h + 1) * DN_DH]
        gcol_[un] = gc[r, DN_HEADS + h:DN_HEADS + h + 1]
        g_row = gct[DN_HEADS + h:DN_HEADS + h + 1, r]
        bcol_[un] = beta[r, h:h + 1]
        glast_[un] = gc[(c + 1) * CHUNK - 1:(c + 1) * CHUNK, DN_HEADS + h:DN_HEADS + h + 1]
        gamma_[un] = jnp.exp(jnp.where(incl, gcol_[un] - g_row, -jnp.inf))

    qk_kk = {}
    for un in units:
        qb, kb = q_[un].astype(BF16), k_[un].astype(BF16)
        qk_kk[un] = _dot_nt(jnp.concatenate([qb, kb], axis=0), kb)
    a_ = {un: jnp.where(strict, bcol_[un] * qk_kk[un][CHUNK:] * gamma_[un], 0.0) for un in units}

    apow = {un: _mm3(a_[un], a_[un]) for un in units}
    inv = {un: (eye - a_[un]) + _mm3(eye - a_[un], apow[un]) for un in units}
    for _ in range(4):
        apow = {un: _dot(apow[un].astype(BF16), apow[un].astype(BF16)) for un in units}
        inv = {un: inv[un] + _dot(inv[un].astype(BF16), apow[un].astype(BF16)) for un in units}

    eg_ = {un: jnp.exp(gcol_[un]) for un in units}
    rhs_ = {un: jnp.concatenate([v_[un] * bcol_[un], k_[un] * (bcol_[un] * eg_[un])], axis=1)
            for un in units}
    sol_ = {un: rhs_[un] + _mm3(inv[un] - eye, rhs_[un]) for un in units}
    wq_ = {un: jnp.concatenate([sol_[un][:, DN_DH:], q_[un] * eg_[un]], axis=0).astype(BF16)
           for un in units}
    qkm_ = {un: jnp.where(incl, qk_kk[un][:CHUNK] * gamma_[un], 0.0).astype(BF16) for un in units}
    kdec_ = {un: (k_[un] * jnp.exp(glast_[un] - gcol_[un])).astype(BF16) for un in units}

    s_cur = [st[...] for st in states]
    for c in range(n_chunks):
        r = slice(c * CHUNK, (c + 1) * CHUNK)
        hs = [(c, h) for h in range(DN_HEADS)]
        ws_qs = [_dot(wq_[un], s_cur[un[1]].astype(BF16)) for un in hs]
        vn = [(sol_[un][:, :DN_DH] - ws_qs[un[1]][:CHUNK]).astype(BF16) for un in hs]
        o_ = [ws_qs[un[1]][CHUNK:] + _dot(qkm_[un], vn[un[1]]) for un in hs]
        s_cur = [s_cur[un[1]] * jnp.exp(glast_[un]) + _dot_tn(kdec_[un], vn[un[1]]) for un in hs]
        for h in range(DN_HEADS):
            o = o_[h]
            zz = z_ref[r, h * DN_DH:(h + 1) * DN_DH].astype(F32)
            on = o * lax.rsqrt(jnp.mean(o * o, axis=-1, keepdims=True) + EPS) * ng_ref[...]
            o_ref[r, h * DN_DH:(h + 1) * DN_DH] = (on * _silu(zz)).astype(BF16)
    for h in range(DN_HEADS):
        states[h][...] = s_cur[h]


def _deltanet(dn, ba, z, conv_w, alog_row, dtb_row, norm_g, b, s, tb):
    t = b * s
    nt = s // tb
    row = lambda w: pl.BlockSpec((tb, w), lambda bi, ti: (bi * nt + ti, 0))
    full = lambda a: pl.BlockSpec(a.shape, lambda bi, ti: (0, 0))
    return pl.pallas_call(
        functools.partial(_deltanet_kernel, tb=tb),
        grid=(b, nt),
        in_specs=[row(3 * DN_W), row(LANES), row(DN_W), full(conv_w), full(alog_row),
                  full(dtb_row), full(norm_g)],
        out_specs=row(DN_W),
        out_shape=jax.ShapeDtypeStruct((t, DN_W), BF16),
        scratch_shapes=[
            pltpu.VMEM((tb + SUBLANES, 3 * DN_W), F32),
            pltpu.VMEM((tb, 3 * DN_W), F32),
        ] + [pltpu.VMEM((DN_DH, DN_DH), F32) for _ in range(DN_HEADS)],
        compiler_params=pltpu.CompilerParams(
            dimension_semantics=("arbitrary", "arbitrary"), vmem_limit_bytes=VMEM_LIMIT),
        name="deltanet",
    )(dn, ba, z, conv_w, alog_row, dtb_row, norm_g)


def _sbattn_kernel(q_ref, k_ref, v_ref, o_ref, *, tq, pairs):
    qi = pl.program_id(1)
    lane = lax.broadcasted_iota(I32, (tq, LANES), 1)
    first = lane < SB_DH
    zq = jnp.zeros((tq, LANES), BF16)
    q_st = []
    for p in range(pairs):
        q = q_ref[:, p * LANES:(p + 1) * LANES]
        q_st.append(jnp.concatenate([jnp.where(first, q, zq), jnp.where(first, zq, q)], axis=0))
    ri = lax.broadcasted_iota(I32, (2 * tq, tq), 0)
    ci = lax.broadcasted_iota(I32, (2 * tq, tq), 1)
    causal = ci < jnp.where(ri >= tq, ri - tq, ri)
    ui = lax.broadcasted_iota(I32, (tq, tq), 0)
    uj = lax.broadcasted_iota(I32, (tq, tq), 1)
    upper = (ui > uj).astype(BF16)
    upper2 = jnp.concatenate([upper, upper], axis=0)

    def step(kb, carry, masked):
        k0 = pl.multiple_of(kb * tq, tq)
        ps = range(pairs)
        z = [_dot_nt(q_st[p], k_ref[pl.ds(k0, tq), p * LANES:(p + 1) * LANES]) for p in ps]
        sp = [jnp.maximum(z[p], 0.0) + jnp.log(1.0 + jnp.exp(-jnp.abs(z[p]))) for p in ps]
        l = [jnp.where(causal, sp[p], 0.0) for p in ps] if masked else sp
        later = [_dot(jnp.concatenate(_split2(l[p]), axis=1), upper2) for p in ps]
        a_cat, v_cat = [], []
        for p in ps:
            a = jnp.exp((z[p] - sp[p]) - later[p] - carry[p][0])
            if masked:
                a = jnp.where(causal, a, 0.0)
            a = a.astype(BF16)
            a_cat.append(jnp.concatenate([a[:tq], a[tq:]], axis=1))
            vblk = v_ref[pl.ds(k0, tq), p * LANES:(p + 1) * LANES]
            v_cat.append(jnp.concatenate(
                [jnp.where(first, vblk, zq), jnp.where(first, zq, vblk)], axis=0))
        return tuple(
            (carry[p][0] + jnp.sum(l[p], axis=1, keepdims=True),
             carry[p][1] + _dot(a_cat[p], v_cat[p])) for p in ps)

    init = tuple((jnp.zeros((2 * tq, 1), F32), jnp.zeros((tq, LANES), F32)) for _ in range(pairs))
    carry = step(qi, init, True)

    def cond(st):
        kb, c = st
        r_min = jnp.min(c[0][0])
        for p in range(1, pairs):
            r_min = jnp.minimum(r_min, jnp.min(c[p][0]))
        return (kb >= 0) & (r_min < F32_EXP_ZERO)

    def body(st):
        kb, c = st
        return kb - 1, step(kb, c, False)

    _, carry = lax.while_loop(cond, body, (qi - 1, carry))
    for p in range(pairs):
        o_ref[:, p * LANES:(p + 1) * LANES] = carry[p][1].astype(BF16)


def _sbattn(sb, b, s, tq):
    t = b * s
    nq = s // tq
    pairs = SB_HEADS // 2
    return pl.pallas_call(
        functools.partial(_sbattn_kernel, tq=tq, pairs=pairs),
        grid=(b, nq),
        in_specs=[
            pl.BlockSpec((tq, SB_W), lambda bi, qi: (bi * nq + qi, 0)),
            pl.BlockSpec((s, SB_W), lambda bi, qi: (bi, 1)),
            pl.BlockSpec((s, SB_W), lambda bi, qi: (bi, 2)),
        ],
        out_specs=pl.BlockSpec((tq, SB_W), lambda bi, qi: (bi * nq + qi, 0)),
        out_shape=jax.ShapeDtypeStruct((t, SB_W), BF16),
        compiler_params=pltpu.CompilerParams(
            dimension_semantics=("arbitrary", "arbitrary"), vmem_limit_bytes=VMEM_LIMIT),
        name="sbattn",
    )(sb, sb, sb)


def _memkv_kernel(mem_ref, g_ref, wkt_ref, wv_ref, kt_ref, v_ref):
    m = mem_ref[0]
    r = lax.rsqrt(jnp.mean(m * m, axis=-1, keepdims=True) + EPS)
    mn = (m * r * g_ref[...]).astype(BF16)
    kt = _dot_nt(wkt_ref[...], mn) * (XA_DH ** -0.5)
    v = _dot(mn, wv_ref[...])
    feat_r = lax.broadcasted_iota(I32, kt.shape, 0) // XA_DH
    feat_c = lax.broadcasted_iota(I32, v.shape, 1) // XA_DH
    for h in range(XA_HEADS):
        kt_ref[0, h] = jnp.where(feat_r == h, kt, 0.0).astype(BF16)
        v_ref[0, h] = jnp.where(feat_c == h, v, 0.0).astype(BF16)


def _memkv(mem, g_mem, wkt, wv):
    b, m, d = mem.shape
    return pl.pallas_call(
        _memkv_kernel,
        grid=(b,),
        in_specs=[
            pl.BlockSpec((1, m, d), lambda i: (i, 0, 0)),
            pl.BlockSpec(g_mem.shape, lambda i: (0, 0)),
            pl.BlockSpec(wkt.shape, lambda i: (0, 0)),
            pl.BlockSpec(wv.shape, lambda i: (0, 0)),
        ],
        out_specs=[
            pl.BlockSpec((1, XA_HEADS, XA_W, m), lambda i: (i, 0, 0, 0)),
            pl.BlockSpec((1, XA_HEADS, m, XA_W), lambda i: (i, 0, 0, 0)),
        ],
        out_shape=[
            jax.ShapeDtypeStruct((b, XA_HEADS, XA_W, m), BF16),
            jax.ShapeDtypeStruct((b, XA_HEADS, m, XA_W), BF16),
        ],
        compiler_params=pltpu.CompilerParams(dimension_semantics=("arbitrary",)),
        name="memkv",
    )(mem, g_mem, wkt, wv)


def _mix_kernel(x_ref, osb_ref, odn_ref, gt_ref, wo_ref, gx_ref, wq_ref, kt_ref, v_ref,
                wxo_ref, gf_ref, wr_hi_ref, wr_lo_ref, br_ref,
                x2_ref, hf_ref, route_ref, cnt_ref, cnt_s, *, tm):
    d = x_ref.shape[1]
    x = x_ref[...]
    ya = _dot(osb_ref[...], wo_ref[0:SB_W, :])
    yb = _dot(odn_ref[...], wo_ref[SB_W:, :])
    ga = _sigmoid(gt_ref[:, 0:d]).astype(F32)
    gb = _sigmoid(gt_ref[:, d:]).astype(F32)
    x1 = x + (ga * ya + gb * yb)

    r1 = lax.rsqrt(jnp.mean(x1 * x1, axis=-1, keepdims=True) + EPS)
    hn = (x1 * r1 * gx_ref[...]).astype(BF16)
    q = _dot(hn, wq_ref[...]).astype(BF16)
    o = jnp.zeros((tm, XA_W), F32)
    for h in range(XA_HEADS):
        sc = _dot(q, kt_ref[0, h])
        e = jnp.exp(sc - jnp.max(sc, axis=-1, keepdims=True))
        p = e / jnp.sum(e, axis=-1, keepdims=True)
        o = o + _dot(p.astype(BF16), v_ref[0, h])
    x2 = x1 + _dot(o.astype(BF16), wxo_ref[...])
    x2_ref[...] = x2

    r2 = lax.rsqrt(jnp.mean(x2 * x2, axis=-1, keepdims=True) + EPS)
    hf = x2 * r2 * gf_ref[...]
    for s8 in range(SUBLANES):
        hf_ref[pl.ds(s8, tm, stride=SUBLANES), :] = hf[:, s8 * LANES:(s8 + 1) * LANES]

    h_hi, h_lo = _split2(hf)
    logits = (_dot(h_hi, wr_hi_ref[...]) + (_dot(h_lo, wr_hi_ref[...]) + _dot(h_hi, wr_lo_ref[...]))
              + br_ref[...])
    lane = lax.broadcasted_iota(I32, (tm, LANES), 1)
    neg = -jnp.inf
    big = jnp.int32(LANES)
    gl = jnp.where(lane < N_GROUPS, logits, neg)
    gmax = jnp.max(gl, axis=-1, keepdims=True)
    g_sel = jnp.min(jnp.where(gl == gmax, lane, big), axis=-1, keepdims=True)
    p_g = 1.0 / jnp.sum(jnp.exp(gl - gmax), axis=-1, keepdims=True)
    lo = N_GROUPS + g_sel * EPG
    el = jnp.where((lane >= lo) & (lane < lo + EPG), logits, neg)
    v1 = jnp.max(el, axis=-1, keepdims=True)
    i1 = jnp.min(jnp.where(el == v1, lane, big), axis=-1, keepdims=True)
    el2 = jnp.where(lane == i1, neg, el)
    v2 = jnp.max(el2, axis=-1, keepdims=True)
    i2 = jnp.min(jnp.where(el2 == v2, lane, big), axis=-1, keepdims=True)
    e21 = jnp.exp(v2 - v1)
    w1 = p_g / (1.0 + e21)
    w2 = p_g * e21 / (1.0 + e21)

    @pl.when(pl.program_id(0) == 0)
    def _():
        cnt_s[...] = jnp.zeros_like(cnt_s)

    id1 = i1 - N_GROUPS
    id2 = i2 - N_GROUPS
    hit1 = lane == id1
    hit2 = lane == id2
    member = (hit1 | hit2).astype(BF16)
    ti = lax.broadcasted_iota(I32, (tm, tm), 0)
    tj = lax.broadcasted_iota(I32, (tm, tm), 1)
    before = _dot((tj < ti).astype(BF16), member) + cnt_s[...]
    rank1 = jnp.sum(jnp.where(hit1, before, 0.0), axis=-1, keepdims=True)
    rank2 = jnp.sum(jnp.where(hit2, before, 0.0), axis=-1, keepdims=True)
    cnt_new = cnt_s[...] + jnp.sum(member.astype(F32), axis=0, keepdims=True)
    cnt_s[...] = cnt_new
    cnt_ref[...] = cnt_new

    route = jnp.where(lane == 0, id1.astype(F32),
            jnp.where(lane == 1, id2.astype(F32),
            jnp.where(lane == 2, w1,
            jnp.where(lane == 3, w2,
            jnp.where(lane == 4, rank1, jnp.where(lane == 5, rank2, 0.0))))))
    route_ref[...] = route


def _mix(x2d, osb, odn, gates, wo, gx, wq, kt, v, wxo, gf, wr_hi, wr_lo, br, s, tm):
    t, d = x2d.shape
    nb = s // tm
    row = lambda w: pl.BlockSpec((tm, w), lambda i: (i, 0))
    full = lambda a: pl.BlockSpec(a.shape, lambda i: (0,) * a.ndim)
    per_b = lambda a: pl.BlockSpec((1,) + a.shape[1:], lambda i: (i // nb, 0, 0, 0))
    return pl.pallas_call(
        functools.partial(_mix_kernel, tm=tm),
        grid=(t // tm,),
        in_specs=[row(d), row(SB_W), row(DN_W), row(2 * d), full(wo), full(gx), full(wq),
                  per_b(kt), per_b(v), full(wxo), full(gf), full(wr_hi), full(wr_lo), full(br)],
        out_specs=[row(d), pl.BlockSpec((tm * SUBLANES, LANES), lambda i: (i, 0)), row(LANES),
                   pl.BlockSpec((1, LANES), lambda i: (0, 0))],
        out_shape=[
            jax.ShapeDtypeStruct((t, d), F32),
            jax.ShapeDtypeStruct((t * SUBLANES, LANES), F32),
            jax.ShapeDtypeStruct((t, LANES), F32),
            jax.ShapeDtypeStruct((1, LANES), F32),
        ],
        scratch_shapes=[pltpu.VMEM((1, LANES), F32)],
        compiler_params=pltpu.CompilerParams(
            dimension_semantics=("arbitrary",), vmem_limit_bytes=VMEM_LIMIT),
        name="mix",
    )(x2d, osb, odn, gates, wo, gx, wq, kt, v, wxo, gf, wr_hi, wr_lo, br)


def _row_copy(src_hbm, dst, src_row, dst_row, sem):
    return pltpu.make_async_copy(
        src_hbm.at[pl.ds(src_row * SUBLANES, SUBLANES), :],
        dst.at[pl.ds(dst_row * SUBLANES, SUBLANES), :], sem)


def _gather_ring(i, n_valid, idx_hbm, idx_s, src_hbm, bufs, sem_i, sem_g, rows, consume):
    def idx_copy(tile, sl):
        return pltpu.make_async_copy(idx_hbm.at[tile], idx_s[sl], sem_i.at[sl])

    def wait_rows(sl):
        pltpu.make_async_copy(src_hbm.at[pl.ds(0, rows * SUBLANES), :], bufs[sl],
                              sem_g.at[sl]).wait()

    @pl.when((i == 0) & (n_valid > 0))
    def _():
        idx_copy(0, 0).start()
        idx_copy(0, 0).wait()

        def body(r, c):
            _row_copy(src_hbm, bufs[0], idx_s[0][r], r, sem_g.at[0]).start()
            return c

        lax.fori_loop(0, rows, body, 0, unroll=8)

        @pl.when(n_valid > 1)
        def _():
            idx_copy(1, 1).start()

    parity = lax.rem(i, 2)
    for sl in range(2):
        nx = 1 - sl

        @pl.when((parity == sl) & (i + 1 < n_valid))
        def _(sl=sl, nx=nx):
            idx_copy(i + 1, nx).wait()

            @pl.when(i + 2 < n_valid)
            def _():
                idx_copy(i + 2, sl).start()

            wait_rows(sl)
            for r in range(rows):
                _row_copy(src_hbm, bufs[nx], idx_s[nx][r], r, sem_g.at[nx]).start()
            consume(bufs[sl])

        @pl.when((parity == sl) & (i + 1 == n_valid))
        def _(sl=sl):
            wait_rows(sl)
            consume(bufs[sl])


def _experts_kernel(tile_e_ref, nused_ref, tok_hbm, hf_hbm, wgu_ref, wd_ref, y_ref,
                    tok_s0, tok_s1, xg0, xg1, sem_i, sem_g, *, tm):
    i = pl.program_id(0)
    n_used = nused_ref[0]

    @pl.when(i >= n_used)
    def _():
        y_ref[...] = jnp.zeros_like(y_ref)

    def consume(xs):
        x = jnp.concatenate(
            [xs[pl.ds(s8, tm, stride=SUBLANES), :] for s8 in range(SUBLANES)], axis=1).astype(BF16)
        gu = _dot(x, wgu_ref[0])
        hmid = (_silu(gu[:, :D_EXPERT]) * gu[:, D_EXPERT:]).astype(BF16)
        y = _dot(hmid, wd_ref[0])
        for s8 in range(SUBLANES):
            y_ref[pl.ds(s8, tm, stride=SUBLANES), :] = y[:, s8 * LANES:(s8 + 1) * LANES]

    _gather_ring(i, n_used, tok_hbm, (tok_s0, tok_s1), hf_hbm, (xg0, xg1), sem_i, sem_g, tm,
                 consume)


def _experts(tile_e, n_used, row_tok, hf3, wgu, wd, tm):
    n_tiles = row_tok.shape[0]
    d = wd.shape[2]
    grid_spec = pltpu.PrefetchScalarGridSpec(
        num_scalar_prefetch=2,
        grid=(n_tiles,),
        in_specs=[
            pl.BlockSpec(memory_space=pl.ANY),
            pl.BlockSpec(memory_space=pl.ANY),
            pl.BlockSpec((1,) + wgu.shape[1:], lambda i, te, nu: (te[i], 0, 0)),
            pl.BlockSpec((1,) + wd.shape[1:], lambda i, te, nu: (te[i], 0, 0)),
        ],
        out_specs=pl.BlockSpec((tm * SUBLANES, LANES), lambda i, te, nu: (i, 0)),
        scratch_shapes=[
            pltpu.SMEM((tm,), I32),
            pltpu.SMEM((tm,), I32),
            pltpu.VMEM((tm * SUBLANES, LANES), F32),
            pltpu.VMEM((tm * SUBLANES, LANES), F32),
            pltpu.SemaphoreType.DMA((2,)),
            pltpu.SemaphoreType.DMA((2,)),
        ],
    )
    return pl.pallas_call(
        functools.partial(_experts_kernel, tm=tm),
        grid_spec=grid_spec,
        out_shape=jax.ShapeDtypeStruct((n_tiles * tm * SUBLANES, LANES), F32),
        compiler_params=pltpu.CompilerParams(
            dimension_semantics=("arbitrary",), vmem_limit_bytes=VMEM_LIMIT),
        name="experts",
    )(tile_e, n_used, row_tok, hf3, wgu, wd)


def _combine_kernel(dest_hbm, y_hbm, x2_ref, route_ref, gfin_ref, o_ref,
                    dest_s0, dest_s1, yg0, yg1, sem_i, sem_g, *, tm):
    def consume(ys):
        route = route_ref[...]
        w1 = route[:, 2:3]
        w2 = route[:, 3:4]
        pieces = []
        ss = jnp.zeros((tm, 1), F32)
        for s8 in range(SUBLANES):
            y1 = ys[pl.ds(s8, tm, stride=TOP_K * SUBLANES), :]
            y2 = ys[pl.ds(SUBLANES + s8, tm, stride=TOP_K * SUBLANES), :]
            xo = x2_ref[:, s8 * LANES:(s8 + 1) * LANES] + (y1 * w1 + y2 * w2)
            ss = ss + jnp.sum(xo * xo, axis=-1, keepdims=True)
            pieces.append(xo)
        d = x2_ref.shape[1]
        r = lax.rsqrt(ss * (1.0 / d) + EPS)
        for s8 in range(SUBLANES):
            o_ref[:, s8 * LANES:(s8 + 1) * LANES] = (
                pieces[s8] * r * gfin_ref[:, s8 * LANES:(s8 + 1) * LANES])

    _gather_ring(pl.program_id(0), pl.num_programs(0), dest_hbm, (dest_s0, dest_s1), y_hbm,
                 (yg0, yg1), sem_i, sem_g, TOP_K * tm, consume)


def _combine(slot_dest, ybuf, x2, route, g_final, tm):
    t, d = x2.shape
    row = lambda w: pl.BlockSpec((tm, w), lambda i: (i, 0))
    return pl.pallas_call(
        functools.partial(_combine_kernel, tm=tm),
        grid=(t // tm,),
        in_specs=[
            pl.BlockSpec(memory_space=pl.ANY),
            pl.BlockSpec(memory_space=pl.ANY),
            row(d), row(LANES),
            pl.BlockSpec(g_final.shape, lambda i: (0, 0)),
        ],
        out_specs=row(d),
        out_shape=jax.ShapeDtypeStruct((t, d), F32),
        scratch_shapes=[
            pltpu.SMEM((TOP_K * tm,), I32),
            pltpu.SMEM((TOP_K * tm,), I32),
            pltpu.VMEM((TOP_K * tm * SUBLANES, LANES), F32),
            pltpu.VMEM((TOP_K * tm * SUBLANES, LANES), F32),
            pltpu.SemaphoreType.DMA((2,)),
            pltpu.SemaphoreType.DMA((2,)),
        ],
        compiler_params=pltpu.CompilerParams(
            dimension_semantics=("arbitrary",), vmem_limit_bytes=VMEM_LIMIT),
        name="combine",
    )(slot_dest, ybuf, x2, route, g_final)


def _plan_dispatch(route, counts_row, moe_tm):
    t = route.shape[0]
    n = t * TOP_K
    n_tiles = n // moe_tm + N_EXPERTS
    e = route[:, 0:TOP_K].astype(I32)
    rank = route[:, 4:4 + TOP_K].astype(I32)
    counts = counts_row[0, :N_EXPERTS].astype(I32)
    padded = (counts + moe_tm - 1) // moe_tm * moe_tm
    p_end = jnp.cumsum(padded)
    p_start = p_end - padded
    start = jnp.cumsum(counts) - counts
    slot_dest = p_start[e] + rank
    tile_start = jnp.arange(n_tiles, dtype=I32) * moe_tm
    tile_e = jnp.minimum(jnp.sum(p_end[None, :] <= tile_start[:, None], axis=1), N_EXPERTS - 1)
    tile_e = tile_e.astype(I32)
    n_used = (p_end[-1] // moe_tm).astype(I32).reshape(1)
    order = jnp.argsort(e.reshape(n), stable=True).astype(I32)
    off = jnp.arange(moe_tm, dtype=I32)[None, :] + (tile_start - p_start[tile_e])[:, None]
    valid = off < counts[tile_e][:, None]
    src = jnp.clip(start[tile_e][:, None] + off, 0, n - 1)
    row_tok = jnp.where(valid, order[src] // TOP_K, 0)
    return tile_e, n_used, row_tok, slot_dest


def _lane_row(vals, offset):
    row = jnp.zeros((1, LANES), F32)
    return row.at[0, offset:offset + vals.shape[0]].set(vals.astype(F32))


def _forward(x, mem, g_mix, w_in, dn_conv_w, dn_a_log, dn_dt_bias, dn_norm_g, w_out,
             g_xattn, g_mem, xa_wq, xa_wkv, xa_wo, g_ffn, w_grp, b_grp, w_exp, b_exp,
             we_gate, we_up, we_down, g_final, *, tm_in, tb_dn, tq_sb, tm_mix, moe_tm, tm_out):
    b, s, d = x.shape
    t = b * s
    x2d = x.reshape(t, d)
    assert g_mix.shape[0] == 1, "the final RMSNorm is fused into the (single) layer's combine step"
    for l in range(1):
        w = w_in[l]
        o0 = 3 * SB_W
        o1 = o0 + 3 * DN_W
        o2 = o1 + 2 * DN_HEADS
        o3 = o2 + DN_W
        wsb = jnp.concatenate([w[:, :SB_W] * (SB_DH ** -0.5), w[:, SB_W:o0]], axis=1).astype(BF16)
        wdn = w[:, o0:o1].astype(BF16)
        wba = jnp.pad(w[:, o1:o2], ((0, 0), (0, LANES - 2 * DN_HEADS))).astype(BF16)
        wz = w[:, o2:o3].astype(BF16)
        wg = w[:, o3:].astype(BF16)
        sb, dn, ba, z, gates = _inproj(x2d, g_mix[l][None, :], wsb, wdn, wba, wz, wg, tm_in)

        o_sb = _sbattn(sb, b, s, tq_sb)
        o_dn = _deltanet(dn, ba, z, dn_conv_w[l], _lane_row(dn_a_log[l], DN_HEADS),
                         _lane_row(dn_dt_bias[l], DN_HEADS), dn_norm_g[l][None, :], b, s, tb_dn)

        wkv = xa_wkv[l]
        kt, v = _memkv(mem, g_mem[l][None, :], wkv[:, :XA_W].T.astype(BF16),
                       wkv[:, XA_W:].astype(BF16))

        w_route = jnp.pad(jnp.concatenate([w_grp[l], w_exp[l]], axis=1),
                          ((0, 0), (0, LANES - N_GROUPS - N_EXPERTS)))
        wr_hi = w_route.astype(BF16)
        wr_lo = (w_route - wr_hi.astype(F32)).astype(BF16)
        b_route = _lane_row(jnp.concatenate([b_grp[l], b_exp[l]]), 0)
        x2, hf3, route, counts = _mix(x2d, o_sb, o_dn, gates, w_out[l].astype(BF16),
                                      g_xattn[l][None, :], xa_wq[l].astype(BF16), kt, v,
                                      xa_wo[l].astype(BF16), g_ffn[l][None, :], wr_hi, wr_lo,
                                      b_route, s, tm_mix)

        tile_e, n_used, row_tok, slot_dest = _plan_dispatch(route, counts, moe_tm)
        wgu = jnp.concatenate([we_gate[l], we_up[l]], axis=2).astype(BF16)
        ybuf = _experts(tile_e, n_used, row_tok, hf3, wgu, we_down[l].astype(BF16), moe_tm)
        x2d = _combine(slot_dest.reshape(t // tm_out, TOP_K * tm_out), ybuf, x2, route,
                       g_final[None, :], tm_out)
    return x2d.reshape(b, s, d)


def kernel(x, mem, g_mix, w_in, dn_conv_w, dn_a_log, dn_dt_bias, dn_norm_g, w_out, g_xattn, g_mem,
           xa_wq, xa_wkv, xa_wo, g_ffn, w_grp, b_grp, w_exp, b_exp, we_gate, we_up, we_down, g_final):
    assert g_mix.shape[0] == 1, "single-layer block"
    return _forward(x, mem, g_mix, w_in, dn_conv_w, dn_a_log, dn_dt_bias, dn_norm_g, w_out,
                    g_xattn, g_mem, xa_wq, xa_wkv, xa_wo, g_ffn, w_grp, b_grp, w_exp, b_exp,
                    we_gate, we_up, we_down, g_final,
                    tm_in=256, tb_dn=256, tq_sb=128, tm_mix=256, moe_tm=512, tm_out=256)
```

```python
import functools

import jax
import jax.numpy as jnp
from jax import lax
from jax.experimental import pallas as pl
from jax.experimental.pallas import tpu as pltpu

F32 = jnp.float32
BF16 = jnp.bfloat16
I32 = jnp.int32

EPS = 1e-6
LANES = 128
SUBLANES = 8

SB_HEADS, SB_DH = 8, 64
SB_W = SB_HEADS * SB_DH
DN_HEADS, DN_DH = 4, 128
DN_W = DN_HEADS * DN_DH
DN_CONV = 4
CHUNK = 64
XA_HEADS, XA_DH = 4, 64
XA_W = XA_HEADS * XA_DH
N_GROUPS, EPG = 4, 8
N_EXPERTS = N_GROUPS * EPG
TOP_K = 2
D_EXPERT = 512

F32_EXP_ZERO = 104.0

VMEM_LIMIT = 56 * 1024 * 1024


def _dot(a, b):
    return jnp.dot(a, b, preferred_element_type=F32)


def _dot_nt(a, b):
    return lax.dot_general(a, b, (((1,), (1,)), ((), ())), preferred_element_type=F32)


def _dot_tn(a, b):
    return lax.dot_general(a, b, (((0,), (0,)), ((), ())), preferred_element_type=F32)


def _split2(a):
    hi = a.astype(BF16)
    lo = (a - hi.astype(F32)).astype(BF16)
    return hi, lo


def _split3(a):
    hi = a.astype(BF16)
    r = a - hi.astype(F32)
    mid = r.astype(BF16)
    lo = (r - mid.astype(F32)).astype(BF16)
    return hi, mid, lo


def _mm3(a, b):
    ah, al = _split2(a)
    bh, bl = _split2(b)
    return _dot(ah, bh) + (_dot(ah, bl) + _dot(al, bh))


def _softplus(z):
    return jnp.maximum(z, 0.0) + jnp.log1p(jnp.exp(-jnp.abs(z)))


def _sigmoid(z):
    return 1.0 / (1.0 + jnp.exp(-z))


def _silu(z):
    return z * _sigmoid(z)


def _inproj_kernel(x_ref, g_ref, wsb_ref, wdn_ref, wba_ref, wz_ref, wg_ref, cw_ref,
                   sb_ref, dn_ref, ba_ref, z_ref, gt_ref, cbuf, *, tm, tiles_per_seq):
    x = x_ref[...]
    r = lax.rsqrt(jnp.mean(x * x, axis=-1, keepdims=True) + EPS)
    h = (x * r * g_ref[...]).astype(BF16)
    sb_ref[...] = _dot(h, wsb_ref[...]).astype(BF16)
    ba_ref[...] = _dot(h, wba_ref[...])
    z_ref[...] = _dot(h, wz_ref[...]).astype(BF16)
    gt_ref[...] = _dot(h, wg_ref[...]).astype(BF16)

    @pl.when(pl.program_id(0) % tiles_per_seq == 0)
    def _():
        cbuf[0:SUBLANES, :] = jnp.zeros((SUBLANES, 3 * DN_W), F32)

    cbuf[SUBLANES:SUBLANES + tm, :] = _dot(h, wdn_ref[...])
    conv = cw_ref[0:1, :] * cbuf[SUBLANES - 3:SUBLANES - 3 + tm, :]
    for i in range(1, DN_CONV):
        conv = conv + cw_ref[i:i + 1, :] * cbuf[SUBLANES - 3 + i:SUBLANES - 3 + i + tm, :]
    cbuf[0:SUBLANES, :] = cbuf[tm:tm + SUBLANES, :]
    act = _silu(conv)
    for hh in range(DN_HEADS):
        for part in range(2):
            lo = part * DN_W + hh * DN_DH
            v = act[:, lo:lo + DN_DH]
            n = v * lax.rsqrt(jnp.sum(v * v, axis=-1, keepdims=True) + EPS)
            if part == 0:
                n = n * (DN_DH ** -0.5)
            dn_ref[:, lo:lo + DN_DH] = n
    dn_ref[:, 2 * DN_W:] = act[:, 2 * DN_W:]


def _inproj(x2d, g_mix, wsb, wdn, wba, wz, wg, conv_w, tm, tiles_per_seq):
    t, d = x2d.shape
    full = lambda a: pl.BlockSpec(a.shape, lambda i: (0, 0))
    row = lambda w: pl.BlockSpec((tm, w), lambda i: (i, 0))
    return pl.pallas_call(
        functools.partial(_inproj_kernel, tm=tm, tiles_per_seq=tiles_per_seq),
        grid=(t // tm,),
        in_specs=[row(d), full(g_mix), full(wsb), full(wdn), full(wba), full(wz), full(wg),
                  full(conv_w)],
        scratch_shapes=[pltpu.VMEM((tm + SUBLANES, 3 * DN_W), F32)],
        out_specs=[row(3 * SB_W), row(3 * DN_W), row(LANES), row(DN_W), row(2 * d)],
        out_shape=[
            jax.ShapeDtypeStruct((t, 3 * SB_W), BF16),
            jax.ShapeDtypeStruct((t, 3 * DN_W), F32),
            jax.ShapeDtypeStruct((t, LANES), F32),
            jax.ShapeDtypeStruct((t, DN_W), BF16),
            jax.ShapeDtypeStruct((t, 2 * d), BF16),
        ],
        compiler_params=pltpu.CompilerParams(
            dimension_semantics=("arbitrary",), vmem_limit_bytes=VMEM_LIMIT),
        name="inproj",
    )(x2d, g_mix, wsb, wdn, wba, wz, wg, conv_w)


def _deltanet_kernel(qkv, ba_ref, z_ref, alog_ref, dtb_ref, ng_ref, o_ref,
                     st0, st1, st2, st3, *, tb):
    t_idx = pl.program_id(1)
    n_chunks = tb // CHUNK
    states = (st0, st1, st2, st3)

    @pl.when(t_idx == 0)
    def _():
        for st in states:
            st[...] = jnp.zeros_like(st)

    ba = ba_ref[...]
    beta = _sigmoid(ba)
    g_all = -jnp.exp(alog_ref[...]) * _softplus(ba + dtb_ref[...])
    r_i = lax.broadcasted_iota(I32, (tb, tb), 0)
    c_i = lax.broadcasted_iota(I32, (tb, tb), 1)
    tri = ((r_i // CHUNK == c_i // CHUNK) & (c_i <= r_i)).astype(BF16)
    g3 = _split3(g_all)
    gc = _dot(tri, g3[0]) + (_dot(tri, g3[1]) + _dot(tri, g3[2]))
    gct = gc.T

    ci = lax.broadcasted_iota(I32, (CHUNK, CHUNK), 0)
    cj = lax.broadcasted_iota(I32, (CHUNK, CHUNK), 1)
    incl = cj <= ci
    strict = cj < ci
    eye = (ci == cj).astype(F32)

    units = [(c, h) for c in range(n_chunks) for h in range(DN_HEADS)]
    q_, k_, v_, gcol_, bcol_, glast_, gamma_ = {}, {}, {}, {}, {}, {}, {}
    for un in units:
        c, h = un
        r = slice(c * CHUNK, (c + 1) * CHUNK)
        q_[un] = qkv[r, h * DN_DH:(h + 1) * DN_DH]
        k_[un] = qkv[r, DN_W + h * DN_DH:DN_W + (h + 1) * DN_DH]
        v_[un] = qkv[r, 2 * DN_W + h * DN_DH:2 * DN_W + (h + 1) * DN_DH]
        gcol_[un] = gc[r, DN_HEADS + h:DN_HEADS + h + 1]
        g_row = gct[DN_HEADS + h:DN_HEADS + h + 1, r]
        bcol_[un] = beta[r, h:h + 1]
        glast_[un] = gc[(c + 1) * CHUNK - 1:(c + 1) * CHUNK, DN_HEADS + h:DN_HEADS + h + 1]
        gamma_[un] = jnp.exp(jnp.where(incl, gcol_[un] - g_row, -jnp.inf))

    qk_kk = {}
    for un in units:
        qb, kb = q_[un].astype(BF16), k_[un].astype(BF16)
        qk_kk[un] = _dot_nt(jnp.concatenate([qb, kb], axis=0), kb)
    a_ = {un: jnp.where(strict, bcol_[un] * qk_kk[un][CHUNK:] * gamma_[un], 0.0) for un in units}

    apow = {un: _mm3(a_[un], a_[un]) for un in units}
    inv = {un: (eye - a_[un]) + _mm3(eye - a_[un], apow[un]) for un in units}
    for _ in range(4):
        apow = {un: _dot(apow[un].astype(BF16), apow[un].astype(BF16)) for un in units}
        inv = {un: inv[un] + _dot(inv[un].astype(BF16), apow[un].astype(BF16)) for un in units}

    eg_ = {un: jnp.exp(gcol_[un]) for un in units}
    rhs_ = {un: jnp.concatenate([v_[un] * bcol_[un], k_[un] * (bcol_[un] * eg_[un])], axis=1)
            for un in units}
    sol_ = {un: rhs_[un] + _mm3(inv[un] - eye, rhs_[un]) for un in units}
    wq_ = {un: jnp.concatenate([sol_[un][:, DN_DH:], q_[un] * eg_[un]], axis=0).astype(BF16)
           for un in units}
    qkm_ = {un: jnp.where(incl, qk_kk[un][:CHUNK] * gamma_[un], 0.0).astype(BF16) for un in units}
    kdec_ = {un: (k_[un] * jnp.exp(glast_[un] - gcol_[un])).astype(BF16) for un in units}

    s_cur = [st[...] for st in states]
    for c in range(n_chunks):
        r = slice(c * CHUNK, (c + 1) * CHUNK)
        hs = [(c, h) for h in range(DN_HEADS)]
        ws_qs = [_dot(wq_[un], s_cur[un[1]].astype(BF16)) for un in hs]
        vn = [(sol_[un][:, :DN_DH] - ws_qs[un[1]][:CHUNK]).astype(BF16) for un in hs]
        o_ = [ws_qs[un[1]][CHUNK:] + _dot(qkm_[un], vn[un[1]]) for un in hs]
        s_cur = [s_cur[un[1]] * jnp.exp(glast_[un]) + _dot_tn(kdec_[un], vn[un[1]]) for un in hs]
        for h in range(DN_HEADS):
            o = o_[h]
            zz = z_ref[r, h * DN_DH:(h + 1) * DN_DH].astype(F32)
            on = o * lax.rsqrt(jnp.mean(o * o, axis=-1, keepdims=True) + EPS) * ng_ref[...]
            o_ref[r, h * DN_DH:(h + 1) * DN_DH] = (on * _silu(zz)).astype(BF16)
    for h in range(DN_HEADS):
        states[h][...] = s_cur[h]


def _deltanet(dn, ba, z, alog_row, dtb_row, norm_g, b, s, tb):
    t = b * s
    nt = s // tb
    row = lambda w: pl.BlockSpec((tb, w), lambda bi, ti: (bi * nt + ti, 0))
    full = lambda a: pl.BlockSpec(a.shape, lambda bi, ti: (0, 0))
    return pl.pallas_call(
        functools.partial(_deltanet_kernel, tb=tb),
        grid=(b, nt),
        in_specs=[row(3 * DN_W), row(LANES), row(DN_W), full(alog_row), full(dtb_row),
                  full(norm_g)],
        out_specs=row(DN_W),
        out_shape=jax.ShapeDtypeStruct((t, DN_W), BF16),
        scratch_shapes=[pltpu.VMEM((DN_DH, DN_DH), F32) for _ in range(DN_HEADS)],
        compiler_params=pltpu.CompilerParams(
            dimension_semantics=("arbitrary", "arbitrary"), vmem_limit_bytes=VMEM_LIMIT),
        name="deltanet",
    )(dn, ba, z, alog_row, dtb_row, norm_g)


def _sbattn_kernel(q_ref, k_ref, v_ref, o_ref, *, tq, pairs):
    qi = pl.program_id(1)
    lane = lax.broadcasted_iota(I32, (tq, LANES), 1)
    first = lane < SB_DH
    zq = jnp.zeros((tq, LANES), BF16)
    q_st = []
    for p in range(pairs):
        q = q_ref[:, p * LANES:(p + 1) * LANES]
        q_st.append(jnp.concatenate([jnp.where(first, q, zq), jnp.where(first, zq, q)], axis=0))
    ri = lax.broadcasted_iota(I32, (2 * tq, tq), 0)
    ci = lax.broadcasted_iota(I32, (2 * tq, tq), 1)
    causal = ci < jnp.where(ri >= tq, ri - tq, ri)
    ui = lax.broadcasted_iota(I32, (tq, tq), 0)
    uj = lax.broadcasted_iota(I32, (tq, tq), 1)
    upper = (ui > uj).astype(BF16)

    def step(kb, carry, masked):
        k0 = pl.multiple_of(kb * tq, tq)
        ps = range(pairs)
        z = [_dot_nt(q_st[p], k_ref[pl.ds(k0, tq), p * LANES:(p + 1) * LANES]) for p in ps]
        sp = [jnp.maximum(z[p], 0.0) + jnp.log(1.0 + jnp.exp(-jnp.abs(z[p]))) for p in ps]
        l = [jnp.where(causal, sp[p], 0.0) for p in ps] if masked else sp
        later = [_dot(l[p].astype(BF16), upper) for p in ps]
        a_cat, v_cat = [], []
        for p in ps:
            a = jnp.exp((z[p] - sp[p]) - later[p] - carry[p][0])
            if masked:
                a = jnp.where(causal, a, 0.0)
            a = a.astype(BF16)
            a_cat.append(jnp.concatenate([a[:tq], a[tq:]], axis=1))
            vblk = v_ref[pl.ds(k0, tq), p * LANES:(p + 1) * LANES]
            v_cat.append(jnp.concatenate(
                [jnp.where(first, vblk, zq), jnp.where(first, zq, vblk)], axis=0))
        return tuple(
            (carry[p][0] + jnp.sum(l[p], axis=1, keepdims=True),
             carry[p][1] + _dot(a_cat[p], v_cat[p])) for p in ps)

    init = tuple((jnp.zeros((2 * tq, 1), F32), jnp.zeros((tq, LANES), F32)) for _ in range(pairs))
    carry = step(qi, init, True)

    def cond(st):
        kb, c = st
        r_min = jnp.min(c[0][0])
        for p in range(1, pairs):
            r_min = jnp.minimum(r_min, jnp.min(c[p][0]))
        return (kb >= 0) & (r_min < F32_EXP_ZERO)

    def body(st):
        kb, c = st
        return kb - 1, step(kb, c, False)

    _, carry = lax.while_loop(cond, body, (qi - 1, carry))
    for p in range(pairs):
        o_ref[:, p * LANES:(p + 1) * LANES] = carry[p][1].astype(BF16)


def _sbattn(sb, b, s, tq):
    t = b * s
    nq = s // tq
    pairs = SB_HEADS // 2
    return pl.pallas_call(
        functools.partial(_sbattn_kernel, tq=tq, pairs=pairs),
        grid=(b, nq),
        in_specs=[
            pl.BlockSpec((tq, SB_W), lambda bi, qi: (bi * nq + qi, 0)),
            pl.BlockSpec((s, SB_W), lambda bi, qi: (bi, 1)),
            pl.BlockSpec((s, SB_W), lambda bi, qi: (bi, 2)),
        ],
        out_specs=pl.BlockSpec((tq, SB_W), lambda bi, qi: (bi * nq + qi, 0)),
        out_shape=jax.ShapeDtypeStruct((t, SB_W), BF16),
        compiler_params=pltpu.CompilerParams(
            dimension_semantics=("arbitrary", "arbitrary"), vmem_limit_bytes=VMEM_LIMIT),
        name="sbattn",
    )(sb, sb, sb)


def _memkv_kernel(mem_ref, g_ref, wkt_ref, wv_ref, kt_ref, v_ref):
    m = mem_ref[0]
    r = lax.rsqrt(jnp.mean(m * m, axis=-1, keepdims=True) + EPS)
    mn = (m * r * g_ref[...]).astype(BF16)
    kt = _dot_nt(wkt_ref[...], mn) * (XA_DH ** -0.5)
    v = _dot(mn, wv_ref[...])
    feat_r = lax.broadcasted_iota(I32, kt.shape, 0) // XA_DH
    feat_c = lax.broadcasted_iota(I32, v.shape, 1) // XA_DH
    for h in range(XA_HEADS):
        kt_ref[0, h] = jnp.where(feat_r == h, kt, 0.0).astype(BF16)
        v_ref[0, h] = jnp.where(feat_c == h, v, 0.0).astype(BF16)


def _memkv(mem, g_mem, wkt, wv):
    b, m, d = mem.shape
    return pl.pallas_call(
        _memkv_kernel,
        grid=(b,),
        in_specs=[
            pl.BlockSpec((1, m, d), lambda i: (i, 0, 0)),
            pl.BlockSpec(g_mem.shape, lambda i: (0, 0)),
            pl.BlockSpec(wkt.shape, lambda i: (0, 0)),
            pl.BlockSpec(wv.shape, lambda i: (0, 0)),
        ],
        out_specs=[
            pl.BlockSpec((1, XA_HEADS, XA_W, m), lambda i: (i, 0, 0, 0)),
            pl.BlockSpec((1, XA_HEADS, m, XA_W), lambda i: (i, 0, 0, 0)),
        ],
        out_shape=[
            jax.ShapeDtypeStruct((b, XA_HEADS, XA_W, m), BF16),
            jax.ShapeDtypeStruct((b, XA_HEADS, m, XA_W), BF16),
        ],
        compiler_params=pltpu.CompilerParams(dimension_semantics=("arbitrary",)),
        name="memkv",
    )(mem, g_mem, wkt, wv)


def _mix_kernel(x_ref, osb_ref, odn_ref, gt_ref, wo_ref, gx_ref, wq_ref, kt_ref, v_ref,
                wxo_ref, gf_ref, wr_hi_ref, wr_lo_ref, br_ref,
                x2_ref, hf_ref, route_ref, cnt_ref, cnt_s, *, tm):
    d = x_ref.shape[1]
    x = x_ref[...]
    ya = _dot(osb_ref[...], wo_ref[0:SB_W, :])
    yb = _dot(odn_ref[...], wo_ref[SB_W:, :])
    ga = _sigmoid(gt_ref[:, 0:d]).astype(F32)
    gb = _sigmoid(gt_ref[:, d:]).astype(F32)
    x1 = x + (ga * ya + gb * yb)

    r1 = lax.rsqrt(jnp.mean(x1 * x1, axis=-1, keepdims=True) + EPS)
    hn = (x1 * r1 * gx_ref[...]).astype(BF16)
    q = _dot(hn, wq_ref[...]).astype(BF16)
    o = jnp.zeros((tm, XA_W), F32)
    for h in range(XA_HEADS):
        sc = _dot(q, kt_ref[0, h])
        e = jnp.exp(sc - jnp.max(sc, axis=-1, keepdims=True))
        p = e / jnp.sum(e, axis=-1, keepdims=True)
        o = o + _dot(p.astype(BF16), v_ref[0, h])
    x2 = x1 + _dot(o.astype(BF16), wxo_ref[...])
    x2_ref[...] = x2

    r2 = lax.rsqrt(jnp.mean(x2 * x2, axis=-1, keepdims=True) + EPS)
    hf = x2 * r2 * gf_ref[...]
    for s8 in range(SUBLANES):
        hf_ref[pl.ds(s8, tm, stride=SUBLANES), :] = hf[:, s8 * LANES:(s8 + 1) * LANES]

    h_hi, h_lo = _split2(hf)
    logits = (_dot(h_hi, wr_hi_ref[...]) + (_dot(h_lo, wr_hi_ref[...]) + _dot(h_hi, wr_lo_ref[...]))
              + br_ref[...])
    lane = lax.broadcasted_iota(I32, (tm, LANES), 1)
    neg = -jnp.inf
    big = jnp.int32(LANES)
    gl = jnp.where(lane < N_GROUPS, logits, neg)
    gmax = jnp.max(gl, axis=-1, keepdims=True)
    g_sel = jnp.min(jnp.where(gl == gmax, lane, big), axis=-1, keepdims=True)
    p_g = 1.0 / jnp.sum(jnp.exp(gl - gmax), axis=-1, keepdims=True)
    lo = N_GROUPS + g_sel * EPG
    el = jnp.where((lane >= lo) & (lane < lo + EPG), logits, neg)
    v1 = jnp.max(el, axis=-1, keepdims=True)
    i1 = jnp.min(jnp.where(el == v1, lane, big), axis=-1, keepdims=True)
    el2 = jnp.where(lane == i1, neg, el)
    v2 = jnp.max(el2, axis=-1, keepdims=True)
    i2 = jnp.min(jnp.where(el2 == v2, lane, big), axis=-1, keepdims=True)
    e21 = jnp.exp(v2 - v1)
    w1 = p_g / (1.0 + e21)
    w2 = p_g * e21 / (1.0 + e21)

    @pl.when(pl.program_id(0) == 0)
    def _():
        cnt_s[...] = jnp.zeros_like(cnt_s)

    id1 = i1 - N_GROUPS
    id2 = i2 - N_GROUPS
    hit1 = lane == id1
    hit2 = lane == id2
    member = (hit1 | hit2).astype(BF16)
    ti = lax.broadcasted_iota(I32, (tm, tm), 0)
    tj = lax.broadcasted_iota(I32, (tm, tm), 1)
    before = _dot((tj < ti).astype(BF16), member) + cnt_s[...]
    rank1 = jnp.sum(jnp.where(hit1, before, 0.0), axis=-1, keepdims=True)
    rank2 = jnp.sum(jnp.where(hit2, before, 0.0), axis=-1, keepdims=True)
    cnt_new = cnt_s[...] + jnp.sum(member.astype(F32), axis=0, keepdims=True)
    cnt_s[...] = cnt_new
    cnt_ref[...] = cnt_new

    route = jnp.where(lane == 0, id1.astype(F32),
            jnp.where(lane == 1, id2.astype(F32),
            jnp.where(lane == 2, w1,
            jnp.where(lane == 3, w2,
            jnp.where(lane == 4, rank1, jnp.where(lane == 5, rank2, 0.0))))))
    route_ref[...] = route


def _mix(x2d, osb, odn, gates, wo, gx, wq, kt, v, wxo, gf, wr_hi, wr_lo, br, s, tm):
    t, d = x2d.shape
    nb = s // tm
    row = lambda w: pl.BlockSpec((tm, w), lambda i: (i, 0))
    full = lambda a: pl.BlockSpec(a.shape, lambda i: (0,) * a.ndim)
    per_b = lambda a: pl.BlockSpec((1,) + a.shape[1:], lambda i: (i // nb, 0, 0, 0))
    return pl.pallas_call(
        functools.partial(_mix_kernel, tm=tm),
        grid=(t // tm,),
        in_specs=[row(d), row(SB_W), row(DN_W), row(2 * d), full(wo), full(gx), full(wq),
                  per_b(kt), per_b(v), full(wxo), full(gf), full(wr_hi), full(wr_lo), full(br)],
        out_specs=[row(d), pl.BlockSpec((tm * SUBLANES, LANES), lambda i: (i, 0)), row(LANES),
                   pl.BlockSpec((1, LANES), lambda i: (0, 0))],
        out_shape=[
            jax.ShapeDtypeStruct((t, d), F32),
            jax.ShapeDtypeStruct((t * SUBLANES, LANES), F32),
            jax.ShapeDtypeStruct((t, LANES), F32),
            jax.ShapeDtypeStruct((1, LANES), F32),
        ],
        scratch_shapes=[pltpu.VMEM((1, LANES), F32)],
        compiler_params=pltpu.CompilerParams(
            dimension_semantics=("arbitrary",), vmem_limit_bytes=VMEM_LIMIT),
        name="mix",
    )(x2d, osb, odn, gates, wo, gx, wq, kt, v, wxo, gf, wr_hi, wr_lo, br)


def _row_copy(src_hbm, dst, src_row, dst_row, sem):
    return pltpu.make_async_copy(
        src_hbm.at[pl.ds(src_row * SUBLANES, SUBLANES), :],
        dst.at[pl.ds(dst_row * SUBLANES, SUBLANES), :], sem)


def _gather_ring(i, n_valid, idx_hbm, idx_s, src_hbm, bufs, sem_i, sem_g, rows, consume):
    def idx_copy(tile, sl):
        return pltpu.make_async_copy(idx_hbm.at[tile], idx_s[sl], sem_i.at[sl])

    def wait_rows(sl):
        pltpu.make_async_copy(src_hbm.at[pl.ds(0, rows * SUBLANES), :], bufs[sl],
                              sem_g.at[sl]).wait()

    @pl.when((i == 0) & (n_valid > 0))
    def _():
        idx_copy(0, 0).start()
        idx_copy(0, 0).wait()

        def body(r, c):
            _row_copy(src_hbm, bufs[0], idx_s[0][r], r, sem_g.at[0]).start()
            return c

        lax.fori_loop(0, rows, body, 0, unroll=8)

        @pl.when(n_valid > 1)
        def _():
            idx_copy(1, 1).start()

    parity = lax.rem(i, 2)
    for sl in range(2):
        nx = 1 - sl

        @pl.when((parity == sl) & (i + 1 < n_valid))
        def _(sl=sl, nx=nx):
            idx_copy(i + 1, nx).wait()

            @pl.when(i + 2 < n_valid)
            def _():
                idx_copy(i + 2, sl).start()

            wait_rows(sl)
            for r in range(rows):
                _row_copy(src_hbm, bufs[nx], idx_s[nx][r], r, sem_g.at[nx]).start(priority=r % 2)
            consume(bufs[sl])

        @pl.when((parity == sl) & (i + 1 == n_valid))
        def _(sl=sl):
            wait_rows(sl)
            consume(bufs[sl])


def _experts_kernel(tile_e_ref, nused_ref, tok_hbm, hf_hbm, wgu_ref, wd_ref, y_ref,
                    tok_s0, tok_s1, xg0, xg1, sem_i, sem_g, *, tm):
    i = pl.program_id(0)
    n_used = nused_ref[0]

    @pl.when(i >= n_used)
    def _():
        y_ref[...] = jnp.zeros_like(y_ref)

    def consume(xs):
        x = jnp.concatenate(
            [xs[pl.ds(s8, tm, stride=SUBLANES), :] for s8 in range(SUBLANES)], axis=1).astype(BF16)
        gu = _dot(x, wgu_ref[0])
        hmid = (_silu(gu[:, :D_EXPERT]) * gu[:, D_EXPERT:]).astype(BF16)
        y = _dot(hmid, wd_ref[0])
        for s8 in range(SUBLANES):
            y_ref[pl.ds(s8, tm, stride=SUBLANES), :] = y[:, s8 * LANES:(s8 + 1) * LANES]

    _gather_ring(i, n_used, tok_hbm, (tok_s0, tok_s1), hf_hbm, (xg0, xg1), sem_i, sem_g, tm,
                 consume)


def _experts(tile_e, n_used, row_tok, hf3, wgu, wd, tm):
    n_tiles = row_tok.shape[0]
    d = wd.shape[2]
    grid_spec = pltpu.PrefetchScalarGridSpec(
        num_scalar_prefetch=2,
        grid=(n_tiles,),
        in_specs=[
            pl.BlockSpec(memory_space=pl.ANY),
            pl.BlockSpec(memory_space=pl.ANY),
            pl.BlockSpec((1,) + wgu.shape[1:], lambda i, te, nu: (te[i], 0, 0)),
            pl.BlockSpec((1,) + wd.shape[1:], lambda i, te, nu: (te[i], 0, 0)),
        ],
        out_specs=pl.BlockSpec((tm * SUBLANES, LANES), lambda i, te, nu: (i, 0)),
        scratch_shapes=[
            pltpu.SMEM((tm,), I32),
            pltpu.SMEM((tm,), I32),
            pltpu.VMEM((tm * SUBLANES, LANES), F32),
            pltpu.VMEM((tm * SUBLANES, LANES), F32),
            pltpu.SemaphoreType.DMA((2,)),
            pltpu.SemaphoreType.DMA((2,)),
        ],
    )
    return pl.pallas_call(
        functools.partial(_experts_kernel, tm=tm),
        grid_spec=grid_spec,
        out_shape=jax.ShapeDtypeStruct((n_tiles * tm * SUBLANES, LANES), F32),
        compiler_params=pltpu.CompilerParams(
            dimension_semantics=("arbitrary",), vmem_limit_bytes=VMEM_LIMIT),
        name="experts",
    )(tile_e, n_used, row_tok, hf3, wgu, wd)


def _combine_kernel(dest_hbm, y_hbm, x2_ref, route_ref, gfin_ref, o_ref,
                    dest_s0, dest_s1, yg0, yg1, sem_i, sem_g, *, tm):
    def consume(ys):
        route = route_ref[...]
        w1 = route[:, 2:3]
        w2 = route[:, 3:4]
        pieces = []
        ss = jnp.zeros((tm, 1), F32)
        for s8 in range(SUBLANES):
            y1 = ys[pl.ds(s8, tm, stride=TOP_K * SUBLANES), :]
            y2 = ys[pl.ds(SUBLANES + s8, tm, stride=TOP_K * SUBLANES), :]
            xo = x2_ref[:, s8 * LANES:(s8 + 1) * LANES] + (y1 * w1 + y2 * w2)
            ss = ss + jnp.sum(xo * xo, axis=-1, keepdims=True)
            pieces.append(xo)
        d = x2_ref.shape[1]
        r = lax.rsqrt(ss * (1.0 / d) + EPS)
        for s8 in range(SUBLANES):
            o_ref[:, s8 * LANES:(s8 + 1) * LANES] = (
                pieces[s8] * r * gfin_ref[:, s8 * LANES:(s8 + 1) * LANES])

    _gather_ring(pl.program_id(0), pl.num_programs(0), dest_hbm, (dest_s0, dest_s1), y_hbm,
                 (yg0, yg1), sem_i, sem_g, TOP_K * tm, consume)


def _combine(slot_dest, ybuf, x2, route, g_final, tm):
    t, d = x2.shape
    row = lambda w: pl.BlockSpec((tm, w), lambda i: (i, 0))
    return pl.pallas_call(
        functools.partial(_combine_kernel, tm=tm),
        grid=(t // tm,),
        in_specs=[
            pl.BlockSpec(memory_space=pl.ANY),
            pl.BlockSpec(memory_space=pl.ANY),
            row(d), row(LANES),
            pl.BlockSpec(g_final.shape, lambda i: (0, 0)),
        ],
        out_specs=row(d),
        out_shape=jax.ShapeDtypeStruct((t, d), F32),
        scratch_shapes=[
            pltpu.SMEM((TOP_K * tm,), I32),
            pltpu.SMEM((TOP_K * tm,), I32),
            pltpu.VMEM((TOP_K * tm * SUBLANES, LANES), F32),
            pltpu.VMEM((TOP_K * tm * SUBLANES, LANES), F32),
            pltpu.SemaphoreType.DMA((2,)),
            pltpu.SemaphoreType.DMA((2,)),
        ],
        compiler_params=pltpu.CompilerParams(
            dimension_semantics=("arbitrary",), vmem_limit_bytes=VMEM_LIMIT),
        name="combine",
    )(slot_dest, ybuf, x2, route, g_final)


def _plan_dispatch(route, counts_row, moe_tm):
    t = route.shape[0]
    n = t * TOP_K
    n_tiles = n // moe_tm + N_EXPERTS
    e = route[:, 0:TOP_K].astype(I32)
    rank = route[:, 4:4 + TOP_K].astype(I32)
    counts = counts_row[0, :N_EXPERTS].astype(I32)
    padded = (counts + moe_tm - 1) // moe_tm * moe_tm
    p_end = jnp.cumsum(padded)
    p_start = p_end - padded
    start = jnp.cumsum(counts) - counts
    slot_dest = p_start[e] + rank
    tile_start = jnp.arange(n_tiles, dtype=I32) * moe_tm
    tile_e = jnp.minimum(jnp.sum(p_end[None, :] <= tile_start[:, None], axis=1), N_EXPERTS - 1)
    tile_e = tile_e.astype(I32)
    n_used = (p_end[-1] // moe_tm).astype(I32).reshape(1)
    order = jnp.argsort(e.reshape(n), stable=True).astype(I32)
    off = jnp.arange(moe_tm, dtype=I32)[None, :] + (tile_start - p_start[tile_e])[:, None]
    valid = off < counts[tile_e][:, None]
    src = jnp.clip(start[tile_e][:, None] + off, 0, n - 1)
    row_tok = jnp.where(valid, order[src] // TOP_K, 0)
    return tile_e, n_used, row_tok, slot_dest


def _lane_row(vals, offset):
    row = jnp.zeros((1, LANES), F32)
    return row.at[0, offset:offset + vals.shape[0]].set(vals.astype(F32))


def _forward(x, mem, g_mix, w_in, dn_conv_w, dn_a_log, dn_dt_bias, dn_norm_g, w_out,
             g_xattn, g_mem, xa_wq, xa_wkv, xa_wo, g_ffn, w_grp, b_grp, w_exp, b_exp,
             we_gate, we_up, we_down, g_final, *, tm_in, tb_dn, tq_sb, tm_mix, moe_tm, tm_out):
    b, s, d = x.shape
    t = b * s
    x2d = x.reshape(t, d)
    assert g_mix.shape[0] == 1, "the final RMSNorm is fused into the (single) layer's combine step"
    for l in range(1):
        w = w_in[l]
        o0 = 3 * SB_W
        o1 = o0 + 3 * DN_W
        o2 = o1 + 2 * DN_HEADS
        o3 = o2 + DN_W
        wsb = jnp.concatenate([w[:, :SB_W] * (SB_DH ** -0.5), w[:, SB_W:o0]], axis=1).astype(BF16)
        wdn = w[:, o0:o1].astype(BF16)
        wba = jnp.pad(w[:, o1:o2], ((0, 0), (0, LANES - 2 * DN_HEADS))).astype(BF16)
        wz = w[:, o2:o3].astype(BF16)
        wg = w[:, o3:].astype(BF16)
        sb, dn, ba, z, gates = _inproj(x2d, g_mix[l][None, :], wsb, wdn, wba, wz, wg,
                                       dn_conv_w[l], tm_in, s // tm_in)

        o_sb = _sbattn(sb, b, s, tq_sb)
        o_dn = _deltanet(dn, ba, z, _lane_row(dn_a_log[l], DN_HEADS),
                         _lane_row(dn_dt_bias[l], DN_HEADS), dn_norm_g[l][None, :], b, s, tb_dn)

        wkv = xa_wkv[l]
        kt, v = _memkv(mem, g_mem[l][None, :], wkv[:, :XA_W].T.astype(BF16),
                       wkv[:, XA_W:].astype(BF16))

        w_route = jnp.pad(jnp.concatenate([w_grp[l], w_exp[l]], axis=1),
                          ((0, 0), (0, LANES - N_GROUPS - N_EXPERTS)))
        wr_hi = w_route.astype(BF16)
        wr_lo = (w_route - wr_hi.astype(F32)).astype(BF16)
        b_route = _lane_row(jnp.concatenate([b_grp[l], b_exp[l]]), 0)
        x2, hf3, route, counts = _mix(x2d, o_sb, o_dn, gates, w_out[l].astype(BF16),
                                      g_xattn[l][None, :], xa_wq[l].astype(BF16), kt, v,
                                      xa_wo[l].astype(BF16), g_ffn[l][None, :], wr_hi, wr_lo,
                                      b_route, s, tm_mix)

        tile_e, n_used, row_tok, slot_dest = _plan_dispatch(route, counts, moe_tm)
        wgu = jnp.concatenate([we_gate[l], we_up[l]], axis=2).astype(BF16)
        ybuf = _experts(tile_e, n_used, row_tok, hf3, wgu, we_down[l].astype(BF16), moe_tm)
        x2d = _combine(slot_dest.reshape(t // tm_out, TOP_K * tm_out), ybuf, x2, route,
                       g_final[None, :], tm_out)
    return x2d.reshape(b, s, d)


def kernel(x, mem, g_mix, w_in, dn_conv_w, dn_a_log, dn_dt_bias, dn_norm_g, w_out, g_xattn, g_mem,
           xa_wq, xa_wkv, xa_wo, g_ffn, w_grp, b_grp, w_exp, b_exp, we_gate, we_up, we_down, g_final):
    assert g_mix.shape[0] == 1, "single-layer block"
    return _forward(x, mem, g_mix, w_in, dn_conv_w, dn_a_log, dn_dt_bias, dn_norm_g, w_out,
                    g_xattn, g_mem, xa_wq, xa_wkv, xa_wo, g_ffn, w_grp, b_grp, w_exp, b_exp,
                    we_gate, we_up, we_down, g_final,
                    tm_in=256, tb_dn=256, tq_sb=128, tm_mix=256, moe_tm=512, tm_out=256)
```

```python
import functools

import jax
import jax.numpy as jnp
from jax import lax
from jax.experimental import pallas as pl
from jax.experimental.pallas import tpu as pltpu

F32 = jnp.float32
BF16 = jnp.bfloat16
I32 = jnp.int32

EPS = 1e-6
LANES = 128
SUBLANES = 8

SB_HEADS, SB_DH = 8, 64
SB_W = SB_HEADS * SB_DH
DN_HEADS, DN_DH = 4, 128
DN_W = DN_HEADS * DN_DH
DN_CONV = 4
CHUNK = 64
XA_HEADS, XA_DH = 4, 64
XA_W = XA_HEADS * XA_DH
N_GROUPS, EPG = 4, 8
N_EXPERTS = N_GROUPS * EPG
TOP_K = 2
D_EXPERT = 512

F32_EXP_ZERO = 104.0

VMEM_LIMIT = 56 * 1024 * 1024


def _dot(a, b):
    return jnp.dot(a, b, preferred_element_type=F32)


def _dot_nt(a, b):
    return lax.dot_general(a, b, (((1,), (1,)), ((), ())), preferred_element_type=F32)


def _dot_tn(a, b):
    return lax.dot_general(a, b, (((0,), (0,)), ((), ())), preferred_element_type=F32)


def _split2(a):
    hi = a.astype(BF16)
    lo = (a - hi.astype(F32)).astype(BF16)
    return hi, lo


def _split3(a):
    hi = a.astype(BF16)
    r = a - hi.astype(F32)
    mid = r.astype(BF16)
    lo = (r - mid.astype(F32)).astype(BF16)
    return hi, mid, lo


def _mm3(a, b):
    ah, al = _split2(a)
    bh, bl = _split2(b)
    return _dot(ah, bh) + (_dot(ah, bl) + _dot(al, bh))


def _softplus(z):
    return jnp.maximum(z, 0.0) + jnp.log1p(jnp.exp(-jnp.abs(z)))


def _sigmoid(z):
    return 1.0 / (1.0 + jnp.exp(-z))


def _silu(z):
    return z * _sigmoid(z)


def _inproj_kernel(x_ref, g_ref, wsb_ref, wdn_ref, wba_ref, wz_ref, wg_ref, cw_ref,
                   sb_ref, dn_ref, ba_ref, z_ref, gt_ref, cbuf, *, tm, tiles_per_seq):
    x = x_ref[...]
    r = lax.rsqrt(jnp.mean(x * x, axis=-1, keepdims=True) + EPS)
    h = (x * r * g_ref[...]).astype(BF16)
    sb_ref[...] = _dot(h, wsb_ref[...]).astype(BF16)
    ba_ref[...] = _dot(h, wba_ref[...])
    z_ref[...] = _dot(h, wz_ref[...]).astype(BF16)
    gt_ref[...] = _dot(h, wg_ref[...]).astype(BF16)

    @pl.when(pl.program_id(0) % tiles_per_seq == 0)
    def _():
        cbuf[0:SUBLANES, :] = jnp.zeros((SUBLANES, 3 * DN_W), F32)

    cbuf[SUBLANES:SUBLANES + tm, :] = _dot(h, wdn_ref[...])
    conv = cw_ref[0:1, :] * cbuf[SUBLANES - 3:SUBLANES - 3 + tm, :]
    for i in range(1, DN_CONV):
        conv = conv + cw_ref[i:i + 1, :] * cbuf[SUBLANES - 3 + i:SUBLANES - 3 + i + tm, :]
    cbuf[0:SUBLANES, :] = cbuf[tm:tm + SUBLANES, :]
    act = _silu(conv)
    for hh in range(DN_HEADS):
        for part in range(2):
            lo = part * DN_W + hh * DN_DH
            v = act[:, lo:lo + DN_DH]
            n = v * lax.rsqrt(jnp.sum(v * v, axis=-1, keepdims=True) + EPS)
            if part == 0:
                n = n * (DN_DH ** -0.5)
            dn_ref[:, lo:lo + DN_DH] = n
    dn_ref[:, 2 * DN_W:] = act[:, 2 * DN_W:]


def _inproj(x2d, g_mix, wsb, wdn, wba, wz, wg, conv_w, tm, tiles_per_seq):
    t, d = x2d.shape
    full = lambda a: pl.BlockSpec(a.shape, lambda i: (0, 0))
    row = lambda w: pl.BlockSpec((tm, w), lambda i: (i, 0))
    return pl.pallas_call(
        functools.partial(_inproj_kernel, tm=tm, tiles_per_seq=tiles_per_seq),
        grid=(t // tm,),
        in_specs=[row(d), full(g_mix), full(wsb), full(wdn), full(wba), full(wz), full(wg),
                  full(conv_w)],
        scratch_shapes=[pltpu.VMEM((tm + SUBLANES, 3 * DN_W), F32)],
        out_specs=[row(3 * SB_W), row(3 * DN_W), row(LANES), row(DN_W), row(2 * d)],
        out_shape=[
            jax.ShapeDtypeStruct((t, 3 * SB_W), BF16),
            jax.ShapeDtypeStruct((t, 3 * DN_W), F32),
            jax.ShapeDtypeStruct((t, LANES), F32),
            jax.ShapeDtypeStruct((t, DN_W), BF16),
            jax.ShapeDtypeStruct((t, 2 * d), BF16),
        ],
        compiler_params=pltpu.CompilerParams(
            dimension_semantics=("arbitrary",), vmem_limit_bytes=VMEM_LIMIT),
        name="inproj",
    )(x2d, g_mix, wsb, wdn, wba, wz, wg, conv_w)


def _deltanet_kernel(qkv, ba_ref, z_ref, alog_ref, dtb_ref, ng_ref, o_ref,
                     st0, st1, st2, st3, *, tb):
    t_idx = pl.program_id(1)
    n_chunks = tb // CHUNK
    states = (st0, st1, st2, st3)

    @pl.when(t_idx == 0)
    def _():
        for st in states:
            st[...] = jnp.zeros_like(st)

    ba = ba_ref[...]
    beta = _sigmoid(ba)
    g_all = -jnp.exp(alog_ref[...]) * _softplus(ba + dtb_ref[...])
    r_i = lax.broadcasted_iota(I32, (tb, tb), 0)
    c_i = lax.broadcasted_iota(I32, (tb, tb), 1)
    tri = ((r_i // CHUNK == c_i // CHUNK) & (c_i <= r_i)).astype(BF16)
    g3 = _split3(g_all)
    gc = _dot(tri, g3[0]) + (_dot(tri, g3[1]) + _dot(tri, g3[2]))
    gct = gc.T

    ci = lax.broadcasted_iota(I32, (CHUNK, CHUNK), 0)
    cj = lax.broadcasted_iota(I32, (CHUNK, CHUNK), 1)
    incl = cj <= ci
    strict = cj < ci
    eye = (ci == cj).astype(F32)

    units = [(c, h) for c in range(n_chunks) for h in range(DN_HEADS)]
    q_, k_, v_, gcol_, bcol_, glast_, gamma_ = {}, {}, {}, {}, {}, {}, {}
    for un in units:
        c, h = un
        r = slice(c * CHUNK, (c + 1) * CHUNK)
        q_[un] = qkv[r, h * DN_DH:(h + 1) * DN_DH]
        k_[un] = qkv[r, DN_W + h * DN_DH:DN_W + (h + 1) * DN_DH]
        v_[un] = qkv[r, 2 * DN_W + h * DN_DH:2 * DN_W + (h + 1) * DN_DH]
        gcol_[un] = gc[r, DN_HEADS + h:DN_HEADS + h + 1]
        g_row = gct[DN_HEADS + h:DN_HEADS + h + 1, r]
        bcol_[un] = beta[r, h:h + 1]
        glast_[un] = gc[(c + 1) * CHUNK - 1:(c + 1) * CHUNK, DN_HEADS + h:DN_HEADS + h + 1]
        gamma_[un] = jnp.exp(jnp.where(incl, gcol_[un] - g_row, -jnp.inf))

    qk_kk = {}
    for un in units:
        qb, kb = q_[un].astype(BF16), k_[un].astype(BF16)
        qk_kk[un] = _dot_nt(jnp.concatenate([qb, kb], axis=0), kb)
    a_ = {un: jnp.where(strict, bcol_[un] * qk_kk[un][CHUNK:] * gamma_[un], 0.0) for un in units}

    apow = {un: _mm3(a_[un], a_[un]) for un in units}
    inv = {un: (eye - a_[un]) + _mm3(eye - a_[un], apow[un]) for un in units}
    for _ in range(4):
        apow = {un: _dot(apow[un].astype(BF16), apow[un].astype(BF16)) for un in units}
        inv = {un: inv[un] + _dot(inv[un].astype(BF16), apow[un].astype(BF16)) for un in units}

    eg_ = {un: jnp.exp(gcol_[un]) for un in units}
    rhs_ = {un: jnp.concatenate([v_[un] * bcol_[un], k_[un] * (bcol_[un] * eg_[un])], axis=1)
            for un in units}
    sol_ = {un: rhs_[un] + _mm3(inv[un] - eye, rhs_[un]) for un in units}
    wq_ = {un: jnp.concatenate([sol_[un][:, DN_DH:], q_[un] * eg_[un]], axis=0).astype(BF16)
           for un in units}
    qkm_ = {un: jnp.where(incl, qk_kk[un][:CHUNK] * gamma_[un], 0.0).astype(BF16) for un in units}
    kdec_ = {un: (k_[un] * jnp.exp(glast_[un] - gcol_[un])).astype(BF16) for un in units}

    s_cur = [st[...] for st in states]
    for c in range(n_chunks):
        r = slice(c * CHUNK, (c + 1) * CHUNK)
        hs = [(c, h) for h in range(DN_HEADS)]
        ws_qs = [_dot(wq_[un], s_cur[un[1]].astype(BF16)) for un in hs]
        vn = [(sol_[un][:, :DN_DH] - ws_qs[un[1]][:CHUNK]).astype(BF16) for un in hs]
        o_ = [ws_qs[un[1]][CHUNK:] + _dot(qkm_[un], vn[un[1]]) for un in hs]
        s_cur = [s_cur[un[1]] * jnp.exp(glast_[un]) + _dot_tn(kdec_[un], vn[un[1]]) for un in hs]
        for h in range(DN_HEADS):
            o = o_[h]
            zz = z_ref[r, h * DN_DH:(h + 1) * DN_DH].astype(F32)
            on = o * lax.rsqrt(jnp.mean(o * o, axis=-1, keepdims=True) + EPS) * ng_ref[...]
            o_ref[r, h * DN_DH:(h + 1) * DN_DH] = (on * _silu(zz)).astype(BF16)
    for h in range(DN_HEADS):
        states[h][...] = s_cur[h]


def _deltanet(dn, ba, z, alog_row, dtb_row, norm_g, b, s, tb):
    t = b * s
    nt = s // tb
    row = lambda w: pl.BlockSpec((tb, w), lambda bi, ti: (bi * nt + ti, 0))
    full = lambda a: pl.BlockSpec(a.shape, lambda bi, ti: (0, 0))
    return pl.pallas_call(
        functools.partial(_deltanet_kernel, tb=tb),
        grid=(b, nt),
        in_specs=[row(3 * DN_W), row(LANES), row(DN_W), full(alog_row), full(dtb_row),
                  full(norm_g)],
        out_specs=row(DN_W),
        out_shape=jax.ShapeDtypeStruct((t, DN_W), BF16),
        scratch_shapes=[pltpu.VMEM((DN_DH, DN_DH), F32) for _ in range(DN_HEADS)],
        compiler_params=pltpu.CompilerParams(
            dimension_semantics=("arbitrary", "arbitrary"), vmem_limit_bytes=VMEM_LIMIT),
        name="deltanet",
    )(dn, ba, z, alog_row, dtb_row, norm_g)


def _sbattn_kernel(q_ref, k_ref, v_ref, o_ref, *, tq, pairs):
    qi = pl.program_id(1)
    lane = lax.broadcasted_iota(I32, (tq, LANES), 1)
    first = lane < SB_DH
    zq = jnp.zeros((tq, LANES), BF16)
    q_st = []
    for p in range(pairs):
        q = q_ref[:, p * LANES:(p + 1) * LANES]
        q_st.append(jnp.concatenate([jnp.where(first, q, zq), jnp.where(first, zq, q)], axis=0))
    ri = lax.broadcasted_iota(I32, (2 * tq, tq), 0)
    ci = lax.broadcasted_iota(I32, (2 * tq, tq), 1)
    causal = ci < jnp.where(ri >= tq, ri - tq, ri)
    ui = lax.broadcasted_iota(I32, (tq, tq), 0)
    uj = lax.broadcasted_iota(I32, (tq, tq), 1)
    upper = (ui > uj).astype(BF16)

    def step(kb, carry, masked):
        k0 = pl.multiple_of(kb * tq, tq)
        ps = range(pairs)
        z = [_dot_nt(q_st[p], k_ref[pl.ds(k0, tq), p * LANES:(p + 1) * LANES]) for p in ps]
        sp = [jnp.maximum(z[p], 0.0) + jnp.log(1.0 + jnp.exp(-jnp.abs(z[p]))) for p in ps]
        l = [jnp.where(causal, sp[p], 0.0) for p in ps] if masked else sp
        later = [_dot(l[p].astype(BF16), upper) for p in ps]
        a_cat, v_cat = [], []
        for p in ps:
            a = jnp.exp((z[p] - sp[p]) - later[p] - carry[p][0])
            if masked:
                a = jnp.where(causal, a, 0.0)
            a = a.astype(BF16)
            a_cat.append(jnp.concatenate([a[:tq], a[tq:]], axis=1))
            vblk = v_ref[pl.ds(k0, tq), p * LANES:(p + 1) * LANES]
            v_cat.append(jnp.concatenate(
                [jnp.where(first, vblk, zq), jnp.where(first, zq, vblk)], axis=0))
        return tuple(
            (carry[p][0] + jnp.sum(l[p], axis=1, keepdims=True),
             carry[p][1] + _dot(a_cat[p], v_cat[p])) for p in ps)

    init = tuple((jnp.zeros((2 * tq, 1), F32), jnp.zeros((tq, LANES), F32)) for _ in range(pairs))
    carry = step(qi, init, True)

    def cond(st):
        kb, c = st
        r_min = jnp.min(c[0][0])
        for p in range(1, pairs):
            r_min = jnp.minimum(r_min, jnp.min(c[p][0]))
        return (kb >= 0) & (r_min < F32_EXP_ZERO)

    def body(st):
        kb, c = st
        return kb - 1, step(kb, c, False)

    _, carry = lax.while_loop(cond, body, (qi - 1, carry))
    for p in range(pairs):
        o_ref[:, p * LANES:(p + 1) * LANES] = carry[p][1].astype(BF16)


def _sbattn(sb, b, s, tq):
    t = b * s
    nq = s // tq
    pairs = SB_HEADS // 2
    return pl.pallas_call(
        functools.partial(_sbattn_kernel, tq=tq, pairs=pairs),
        grid=(b, nq),
        in_specs=[
            pl.BlockSpec((tq, SB_W), lambda bi, qi: (bi * nq + qi, 0)),
            pl.BlockSpec((s, SB_W), lambda bi, qi: (bi, 1)),
            pl.BlockSpec((s, SB_W), lambda bi, qi: (bi, 2)),
        ],
        out_specs=pl.BlockSpec((tq, SB_W), lambda bi, qi: (bi * nq + qi, 0)),
        out_shape=jax.ShapeDtypeStruct((t, SB_W), BF16),
        compiler_params=pltpu.CompilerParams(
            dimension_semantics=("arbitrary", "arbitrary"), vmem_limit_bytes=VMEM_LIMIT),
        name="sbattn",
    )(sb, sb, sb)


def _memkv_kernel(mem_ref, g_ref, wkt_ref, wv_ref, kt_ref, v_ref):
    m = mem_ref[0]
    r = lax.rsqrt(jnp.mean(m * m, axis=-1, keepdims=True) + EPS)
    mn = (m * r * g_ref[...]).astype(BF16)
    kt = _dot_nt(wkt_ref[...], mn) * (XA_DH ** -0.5)
    v = _dot(mn, wv_ref[...])
    feat_r = lax.broadcasted_iota(I32, kt.shape, 0) // XA_DH
    feat_c = lax.broadcasted_iota(I32, v.shape, 1) // XA_DH
    for h in range(XA_HEADS):
        kt_ref[0, h] = jnp.where(feat_r == h, kt, 0.0).astype(BF16)
        v_ref[0, h] = jnp.where(feat_c == h, v, 0.0).astype(BF16)


def _memkv(mem, g_mem, wkt, wv):
    b, m, d = mem.shape
    return pl.pallas_call(
        _memkv_kernel,
        grid=(b,),
        in_specs=[
            pl.BlockSpec((1, m, d), lambda i: (i, 0, 0)),
            pl.BlockSpec(g_mem.shape, lambda i: (0, 0)),
            pl.BlockSpec(wkt.shape, lambda i: (0, 0)),
            pl.BlockSpec(wv.shape, lambda i: (0, 0)),
        ],
        out_specs=[
            pl.BlockSpec((1, XA_HEADS, XA_W, m), lambda i: (i, 0, 0, 0)),
            pl.BlockSpec((1, XA_HEADS, m, XA_W), lambda i: (i, 0, 0, 0)),
        ],
        out_shape=[
            jax.ShapeDtypeStruct((b, XA_HEADS, XA_W, m), BF16),
            jax.ShapeDtypeStruct((b, XA_HEADS, m, XA_W), BF16),
        ],
        compiler_params=pltpu.CompilerParams(dimension_semantics=("arbitrary",)),
        name="memkv",
    )(mem, g_mem, wkt, wv)


def _mix_kernel(x_ref, osb_ref, odn_ref, gt_ref, wo_ref, gx_ref, wq_ref, kt_ref, v_ref,
                wxo_ref, gf_ref, wr_hi_ref, wr_lo_ref, br_ref,
                x2_ref, hf_ref, route_ref, cnt_ref, cnt_s, *, tm):
    d = x_ref.shape[1]
    hm = tm // 2
    halves = (slice(0, hm), slice(hm, tm))
    ya = [_dot(osb_ref[r, :], wo_ref[0:SB_W, :]) for r in halves]
    yb = [_dot(odn_ref[r, :], wo_ref[SB_W:, :]) for r in halves]
    x1, hn = [], []
    for j, r in enumerate(halves):
        ga = _sigmoid(gt_ref[r, 0:d]).astype(F32)
        gb = _sigmoid(gt_ref[r, d:]).astype(F32)
        xv = x_ref[r, :] + (ga * ya[j] + gb * yb[j])
        x1.append(xv)
        r1 = lax.rsqrt(jnp.mean(xv * xv, axis=-1, keepdims=True) + EPS)
        hn.append((xv * r1 * gx_ref[...]).astype(BF16))

    q = [_dot(hn[j], wq_ref[...]).astype(BF16) for j in range(2)]
    sc = [[_dot(q[j], kt_ref[0, h]) for h in range(XA_HEADS)] for j in range(2)]
    pr = []
    for j in range(2):
        row = []
        for h in range(XA_HEADS):
            e = jnp.exp(sc[j][h] - jnp.max(sc[j][h], axis=-1, keepdims=True))
            row.append((e / jnp.sum(e, axis=-1, keepdims=True)).astype(BF16))
        pr.append(row)
    o = []
    for j in range(2):
        acc = _dot(pr[j][0], v_ref[0, 0])
        for h in range(1, XA_HEADS):
            acc = acc + _dot(pr[j][h], v_ref[0, h])
        o.append(acc.astype(BF16))
    ca = [_dot(o[j], wxo_ref[...]) for j in range(2)]

    hfs = []
    for j, r in enumerate(halves):
        x2 = x1[j] + ca[j]
        x2_ref[r, :] = x2
        r2 = lax.rsqrt(jnp.mean(x2 * x2, axis=-1, keepdims=True) + EPS)
        hf = x2 * r2 * gf_ref[...]
        for s8 in range(SUBLANES):
            hf_ref[pl.ds(j * hm * SUBLANES + s8, hm, stride=SUBLANES), :] = (
                hf[:, s8 * LANES:(s8 + 1) * LANES])
        hfs.append(_split2(hf))

    logits = jnp.concatenate(
        [_dot(hfs[j][0], wr_hi_ref[...]) + (_dot(hfs[j][1], wr_hi_ref[...])
                                            + _dot(hfs[j][0], wr_lo_ref[...]))
         for j in range(2)], axis=0) + br_ref[...]
    lane = lax.broadcasted_iota(I32, (tm, LANES), 1)
    neg = -jnp.inf
    big = jnp.int32(LANES)
    gl = jnp.where(lane < N_GROUPS, logits, neg)
    gmax = jnp.max(gl, axis=-1, keepdims=True)
    g_sel = jnp.min(jnp.where(gl == gmax, lane, big), axis=-1, keepdims=True)
    p_g = 1.0 / jnp.sum(jnp.exp(gl - gmax), axis=-1, keepdims=True)
    lo = N_GROUPS + g_sel * EPG
    el = jnp.where((lane >= lo) & (lane < lo + EPG), logits, neg)
    v1 = jnp.max(el, axis=-1, keepdims=True)
    i1 = jnp.min(jnp.where(el == v1, lane, big), axis=-1, keepdims=True)
    el2 = jnp.where(lane == i1, neg, el)
    v2 = jnp.max(el2, axis=-1, keepdims=True)
    i2 = jnp.min(jnp.where(el2 == v2, lane, big), axis=-1, keepdims=True)
    e21 = jnp.exp(v2 - v1)
    w1 = p_g / (1.0 + e21)
    w2 = p_g * e21 / (1.0 + e21)

    @pl.when(pl.program_id(0) == 0)
    def _():
        cnt_s[...] = jnp.zeros_like(cnt_s)

    id1 = i1 - N_GROUPS
    id2 = i2 - N_GROUPS
    hit1 = lane == id1
    hit2 = lane == id2
    member = (hit1 | hit2).astype(BF16)
    ti = lax.broadcasted_iota(I32, (tm, tm), 0)
    tj = lax.broadcasted_iota(I32, (tm, tm), 1)
    before = _dot((tj < ti).astype(BF16), member) + cnt_s[...]
    rank1 = jnp.sum(jnp.where(hit1, before, 0.0), axis=-1, keepdims=True)
    rank2 = jnp.sum(jnp.where(hit2, before, 0.0), axis=-1, keepdims=True)
    cnt_new = cnt_s[...] + jnp.sum(member.astype(F32), axis=0, keepdims=True)
    cnt_s[...] = cnt_new
    cnt_ref[...] = cnt_new

    route = jnp.where(lane == 0, id1.astype(F32),
            jnp.where(lane == 1, id2.astype(F32),
            jnp.where(lane == 2, w1,
            jnp.where(lane == 3, w2,
            jnp.where(lane == 4, rank1, jnp.where(lane == 5, rank2, 0.0))))))
    route_ref[...] = route


def _mix(x2d, osb, odn, gates, wo, gx, wq, kt, v, wxo, gf, wr_hi, wr_lo, br, s, tm):
    t, d = x2d.shape
    nb = s // tm
    row = lambda w: pl.BlockSpec((tm, w), lambda i: (i, 0))
    full = lambda a: pl.BlockSpec(a.shape, lambda i: (0,) * a.ndim)
    per_b = lambda a: pl.BlockSpec((1,) + a.shape[1:], lambda i: (i // nb, 0, 0, 0))
    return pl.pallas_call(
        functools.partial(_mix_kernel, tm=tm),
        grid=(t // tm,),
        in_specs=[row(d), row(SB_W), row(DN_W), row(2 * d), full(wo), full(gx), full(wq),
                  per_b(kt), per_b(v), full(wxo), full(gf), full(wr_hi), full(wr_lo), full(br)],
        out_specs=[row(d), pl.BlockSpec((tm * SUBLANES, LANES), lambda i: (i, 0)), row(LANES),
                   pl.BlockSpec((1, LANES), lambda i: (0, 0))],
        out_shape=[
            jax.ShapeDtypeStruct((t, d), F32),
            jax.ShapeDtypeStruct((t * SUBLANES, LANES), F32),
            jax.ShapeDtypeStruct((t, LANES), F32),
            jax.ShapeDtypeStruct((1, LANES), F32),
        ],
        scratch_shapes=[pltpu.VMEM((1, LANES), F32)],
        compiler_params=pltpu.CompilerParams(
            dimension_semantics=("arbitrary",), vmem_limit_bytes=VMEM_LIMIT),
        name="mix",
    )(x2d, osb, odn, gates, wo, gx, wq, kt, v, wxo, gf, wr_hi, wr_lo, br)


def _row_copy(src_hbm, dst, src_row, dst_row, sem):
    return pltpu.make_async_copy(
        src_hbm.at[pl.ds(src_row * SUBLANES, SUBLANES), :],
        dst.at[pl.ds(dst_row * SUBLANES, SUBLANES), :], sem)


def _gather_ring(i, n_valid, idx_hbm, idx_s, src_hbm, bufs, sem_i, sem_g, rows, consume):
    def idx_copy(tile, sl):
        return pltpu.make_async_copy(idx_hbm.at[tile], idx_s[sl], sem_i.at[sl])

    def wait_rows(sl):
        pltpu.make_async_copy(src_hbm.at[pl.ds(0, rows * SUBLANES), :], bufs[sl],
                              sem_g.at[sl]).wait()

    @pl.when((i == 0) & (n_valid > 0))
    def _():
        idx_copy(0, 0).start()
        idx_copy(0, 0).wait()

        def body(r, c):
            _row_copy(src_hbm, bufs[0], idx_s[0][r], r, sem_g.at[0]).start()
            return c

        lax.fori_loop(0, rows, body, 0, unroll=8)

        @pl.when(n_valid > 1)
        def _():
            idx_copy(1, 1).start()

    parity = lax.rem(i, 2)
    for sl in range(2):
        nx = 1 - sl

        @pl.when((parity == sl) & (i + 1 < n_valid))
        def _(sl=sl, nx=nx):
            idx_copy(i + 1, nx).wait()

            @pl.when(i + 2 < n_valid)
            def _():
                idx_copy(i + 2, sl).start()

            wait_rows(sl)
            for r in range(rows):
                _row_copy(src_hbm, bufs[nx], idx_s[nx][r], r, sem_g.at[nx]).start(priority=r % 2)
            consume(bufs[sl])

        @pl.when((parity == sl) & (i + 1 == n_valid))
        def _(sl=sl):
            wait_rows(sl)
            consume(bufs[sl])


def _experts_kernel(tile_e_ref, nused_ref, tok_hbm, hf_hbm, wg_ref, wu_ref, wd_ref, y_ref,
                    tok_s0, tok_s1, xg0, xg1, wgu_s, wd_s, sem_i, sem_g, *, tm):
    i = pl.program_id(0)
    n_used = nused_ref[0]

    @pl.when(i >= n_used)
    def _():
        y_ref[...] = jnp.zeros_like(y_ref)

    new_expert = (i == 0) | (tile_e_ref[i] != tile_e_ref[jnp.maximum(i - 1, 0)])

    @pl.when((i < n_used) & new_expert)
    def _():
        wgu_s[:, :D_EXPERT] = wg_ref[0].astype(BF16)
        wgu_s[:, D_EXPERT:] = wu_ref[0].astype(BF16)
        wd_s[...] = wd_ref[0].astype(BF16)

    def consume(xs):
        x = jnp.concatenate(
            [xs[pl.ds(s8, tm, stride=SUBLANES), :] for s8 in range(SUBLANES)], axis=1).astype(BF16)
        gu = _dot(x, wgu_s[...])
        hmid = (_silu(gu[:, :D_EXPERT]) * gu[:, D_EXPERT:]).astype(BF16)
        y = _dot(hmid, wd_s[...])
        for s8 in range(SUBLANES):
            y_ref[pl.ds(s8, tm, stride=SUBLANES), :] = y[:, s8 * LANES:(s8 + 1) * LANES]

    _gather_ring(i, n_used, tok_hbm, (tok_s0, tok_s1), hf_hbm, (xg0, xg1), sem_i, sem_g, tm,
                 consume)


def _experts(tile_e, n_used, row_tok, hf3, wg, wu, wd, tm):
    n_tiles = row_tok.shape[0]
    d, de = wg.shape[1], wg.shape[2]
    per_expert = lambda a: pl.BlockSpec((1,) + a.shape[1:], lambda i, te, nu: (te[i], 0, 0))
    grid_spec = pltpu.PrefetchScalarGridSpec(
        num_scalar_prefetch=2,
        grid=(n_tiles,),
        in_specs=[
            pl.BlockSpec(memory_space=pl.ANY),
            pl.BlockSpec(memory_space=pl.ANY),
            per_expert(wg), per_expert(wu), per_expert(wd),
        ],
        out_specs=pl.BlockSpec((tm * SUBLANES, LANES), lambda i, te, nu: (i, 0)),
        scratch_shapes=[
            pltpu.SMEM((tm,), I32),
            pltpu.SMEM((tm,), I32),
            pltpu.VMEM((tm * SUBLANES, LANES), F32),
            pltpu.VMEM((tm * SUBLANES, LANES), F32),
            pltpu.VMEM((d, 2 * de), BF16),
            pltpu.VMEM((de, d), BF16),
            pltpu.SemaphoreType.DMA((2,)),
            pltpu.SemaphoreType.DMA((2,)),
        ],
    )
    return pl.pallas_call(
        functools.partial(_experts_kernel, tm=tm),
        grid_spec=grid_spec,
        out_shape=jax.ShapeDtypeStruct((n_tiles * tm * SUBLANES, LANES), F32),
        compiler_params=pltpu.CompilerParams(
            dimension_semantics=("arbitrary",), vmem_limit_bytes=VMEM_LIMIT),
        name="experts",
    )(tile_e, n_used, row_tok, hf3, wg, wu, wd)


def _combine_kernel(dest_hbm, y_hbm, x2_ref, route_ref, gfin_ref, o_ref,
                    dest_s0, dest_s1, yg0, yg1, sem_i, sem_g, *, tm):
    def consume(ys):
        route = route_ref[...]
        w1 = route[:, 2:3]
        w2 = route[:, 3:4]
        pieces = []
        ss = jnp.zeros((tm, 1), F32)
        for s8 in range(SUBLANES):
            y1 = ys[pl.ds(s8, tm, stride=TOP_K * SUBLANES), :]
            y2 = ys[pl.ds(SUBLANES + s8, tm, stride=TOP_K * SUBLANES), :]
            xo = x2_ref[:, s8 * LANES:(s8 + 1) * LANES] + (y1 * w1 + y2 * w2)
            ss = ss + jnp.sum(xo * xo, axis=-1, keepdims=True)
            pieces.append(xo)
        d = x2_ref.shape[1]
        r = lax.rsqrt(ss * (1.0 / d) + EPS)
        for s8 in range(SUBLANES):
            o_ref[:, s8 * LANES:(s8 + 1) * LANES] = (
                pieces[s8] * r * gfin_ref[:, s8 * LANES:(s8 + 1) * LANES])

    _gather_ring(pl.program_id(0), pl.num_programs(0), dest_hbm, (dest_s0, dest_s1), y_hbm,
                 (yg0, yg1), sem_i, sem_g, TOP_K * tm, consume)


def _combine(slot_dest, ybuf, x2, route, g_final, tm):
    t, d = x2.shape
    row = lambda w: pl.BlockSpec((tm, w), lambda i: (i, 0))
    return pl.pallas_call(
        functools.partial(_combine_kernel, tm=tm),
        grid=(t // tm,),
        in_specs=[
            pl.BlockSpec(memory_space=pl.ANY),
            pl.BlockSpec(memory_space=pl.ANY),
            row(d), row(LANES),
            pl.BlockSpec(g_final.shape, lambda i: (0, 0)),
        ],
        out_specs=row(d),
        out_shape=jax.ShapeDtypeStruct((t, d), F32),
        scratch_shapes=[
            pltpu.SMEM((TOP_K * tm,), I32),
            pltpu.SMEM((TOP_K * tm,), I32),
            pltpu.VMEM((TOP_K * tm * SUBLANES, LANES), F32),
            pltpu.VMEM((TOP_K * tm * SUBLANES, LANES), F32),
            pltpu.SemaphoreType.DMA((2,)),
            pltpu.SemaphoreType.DMA((2,)),
        ],
        compiler_params=pltpu.CompilerParams(
            dimension_semantics=("arbitrary",), vmem_limit_bytes=VMEM_LIMIT),
        name="combine",
    )(slot_dest, ybuf, x2, route, g_final)


def _plan_dispatch(route, counts_row, moe_tm):
    t = route.shape[0]
    n = t * TOP_K
    n_tiles = n // moe_tm + N_EXPERTS
    e_cols = [route[:, k].astype(I32) for k in range(TOP_K)]
    counts = counts_row[0, :N_EXPERTS].astype(I32)
    padded = (counts + moe_tm - 1) // moe_tm * moe_tm
    p_end = jnp.cumsum(padded)
    p_start = p_end - padded
    start = jnp.cumsum(counts) - counts
    slot_dest = jnp.stack(
        [p_start[e_cols[k]] + route[:, 4 + k].astype(I32) for k in range(TOP_K)], axis=1)
    e = jnp.stack(e_cols, axis=1)
    tile_start = jnp.arange(n_tiles, dtype=I32) * moe_tm
    tile_e = jnp.minimum(jnp.sum(p_end[None, :] <= tile_start[:, None], axis=1), N_EXPERTS - 1)
    tile_e = tile_e.astype(I32)
    n_used = (p_end[-1] // moe_tm).astype(I32).reshape(1)
    order = jnp.argsort(e.reshape(n), stable=True).astype(I32)
    off = jnp.arange(moe_tm, dtype=I32)[None, :] + (tile_start - p_start[tile_e])[:, None]
    valid = off < counts[tile_e][:, None]
    src = jnp.clip(start[tile_e][:, None] + off, 0, n - 1)
    row_tok = jnp.where(valid, order[src] // TOP_K, 0)
    return tile_e, n_used, row_tok, slot_dest


def _lane_row(vals, offset):
    row = jnp.zeros((1, LANES), F32)
    return row.at[0, offset:offset + vals.shape[0]].set(vals.astype(F32))


def _forward(x, mem, g_mix, w_in, dn_conv_w, dn_a_log, dn_dt_bias, dn_norm_g, w_out,
             g_xattn, g_mem, xa_wq, xa_wkv, xa_wo, g_ffn, w_grp, b_grp, w_exp, b_exp,
             we_gate, we_up, we_down, g_final, *, tm_in, tb_dn, tq_sb, tm_mix, moe_tm, tm_out):
    b, s, d = x.shape
    t = b * s
    x2d = x.reshape(t, d)
    assert g_mix.shape[0] == 1, "the final RMSNorm is fused into the (single) layer's combine step"
    for l in range(1):
        w = w_in[l]
        o0 = 3 * SB_W
        o1 = o0 + 3 * DN_W
        o2 = o1 + 2 * DN_HEADS
        o3 = o2 + DN_W
        wsb = jnp.concatenate([w[:, :SB_W] * (SB_DH ** -0.5), w[:, SB_W:o0]], axis=1).astype(BF16)
        wdn = w[:, o0:o1].astype(BF16)
        wba = jnp.pad(w[:, o1:o2], ((0, 0), (0, LANES - 2 * DN_HEADS))).astype(BF16)
        wz = w[:, o2:o3].astype(BF16)
        wg = w[:, o3:].astype(BF16)
        sb, dn, ba, z, gates = _inproj(x2d, g_mix[l][None, :], wsb, wdn, wba, wz, wg,
                                       dn_conv_w[l], tm_in, s // tm_in)

        o_sb = _sbattn(sb, b, s, tq_sb)
        o_dn = _deltanet(dn, ba, z, _lane_row(dn_a_log[l], DN_HEADS),
                         _lane_row(dn_dt_bias[l], DN_HEADS), dn_norm_g[l][None, :], b, s, tb_dn)

        wkv = xa_wkv[l]
        kt, v = _memkv(mem, g_mem[l][None, :], wkv[:, :XA_W].T.astype(BF16),
                       wkv[:, XA_W:].astype(BF16))

        w_route = jnp.pad(jnp.concatenate([w_grp[l], w_exp[l]], axis=1),
                          ((0, 0), (0, LANES - N_GROUPS - N_EXPERTS)))
        wr_hi = w_route.astype(BF16)
        wr_lo = (w_route - wr_hi.astype(F32)).astype(BF16)
        b_route = _lane_row(jnp.concatenate([b_grp[l], b_exp[l]]), 0)
        x2, hf3, route, counts = _mix(x2d, o_sb, o_dn, gates, w_out[l].astype(BF16),
                                      g_xattn[l][None, :], xa_wq[l].astype(BF16), kt, v,
                                      xa_wo[l].astype(BF16), g_ffn[l][None, :], wr_hi, wr_lo,
                                      b_route, s, tm_mix)

        tile_e, n_used, row_tok, slot_dest = _plan_dispatch(route, counts, moe_tm)
        ybuf = _experts(tile_e, n_used, row_tok, hf3, we_gate[l], we_up[l], we_down[l], moe_tm)
        x2d = _combine(slot_dest.reshape(t // tm_out, TOP_K * tm_out), ybuf, x2, route,
                       g_final[None, :], tm_out)
    return x2d.reshape(b, s, d)


def kernel(x, mem, g_mix, w_in, dn_conv_w, dn_a_log, dn_dt_bias, dn_norm_g, w_out, g_xattn, g_mem,
           xa_wq, xa_wkv, xa_wo, g_ffn, w_grp, b_grp, w_exp, b_exp, we_gate, we_up, we_down, g_final):
    assert g_mix.shape[0] == 1, "single-layer block"
    return _forward(x, mem, g_mix, w_in, dn_conv_w, dn_a_log, dn_dt_bias, dn_norm_g, w_out,
                    g_xattn, g_mem, xa_wq, xa_wkv, xa_wo, g_ffn, w_grp, b_grp, w_exp, b_exp,
                    we_gate, we_up, we_down, g_final,
                    tm_in=256, tb_dn=256, tq_sb=128, tm_mix=512, moe_tm=512, tm_out=256)
```

```python
import functools

import jax
import jax.numpy as jnp
from jax import lax
from jax.experimental import pallas as pl
from jax.experimental.pallas import tpu as pltpu

F32 = jnp.float32
BF16 = jnp.bfloat16
I32 = jnp.int32

EPS = 1e-6
LANES = 128
SUBLANES = 8

SB_HEADS, SB_DH = 8, 64
SB_W = SB_HEADS * SB_DH
DN_HEADS, DN_DH = 4, 128
DN_W = DN_HEADS * DN_DH
DN_CONV = 4
CHUNK = 64
XA_HEADS, XA_DH = 4, 64
XA_W = XA_HEADS * XA_DH
N_GROUPS, EPG = 4, 8
N_EXPERTS = N_GROUPS * EPG
TOP_K = 2
D_EXPERT = 512

F32_EXP_ZERO = 104.0

VMEM_LIMIT = 56 * 1024 * 1024


def _dot(a, b):
    return jnp.dot(a, b, preferred_element_type=F32)


def _dot_nt(a, b):
    return lax.dot_general(a, b, (((1,), (1,)), ((), ())), preferred_element_type=F32)


def _dot_tn(a, b):
    return lax.dot_general(a, b, (((0,), (0,)), ((), ())), preferred_element_type=F32)


def _split2(a):
    hi = a.astype(BF16)
    lo = (a - hi.astype(F32)).astype(BF16)
    return hi, lo


def _split3(a):
    hi = a.astype(BF16)
    r = a - hi.astype(F32)
    mid = r.astype(BF16)
    lo = (r - mid.astype(F32)).astype(BF16)
    return hi, mid, lo


def _mm3(a, b):
    ah, al = _split2(a)
    bh, bl = _split2(b)
    return _dot(ah, bh) + (_dot(ah, bl) + _dot(al, bh))


def _softplus(z):
    return jnp.maximum(z, 0.0) + jnp.log1p(jnp.exp(-jnp.abs(z)))


def _sigmoid(z):
    return 1.0 / (1.0 + jnp.exp(-z))


def _silu(z):
    return z * _sigmoid(z)


def _inproj_kernel(x_ref, g_ref, wsb_ref, wdn_ref, wba_ref, wz_ref, wg_ref, cw_ref,
                   sb_ref, dn_ref, ba_ref, z_ref, gt_ref, cbuf, *, tm, tiles_per_seq):
    x = x_ref[...]
    r = lax.rsqrt(jnp.mean(x * x, axis=-1, keepdims=True) + EPS)
    h = (x * r * g_ref[...]).astype(BF16)
    sb_ref[...] = _dot(h, wsb_ref[...]).astype(BF16)
    ba_ref[...] = _dot(h, wba_ref[...])
    z_ref[...] = _dot(h, wz_ref[...]).astype(BF16)
    gt_ref[...] = _dot(h, wg_ref[...]).astype(BF16)

    @pl.when(pl.program_id(0) % tiles_per_seq == 0)
    def _():
        cbuf[0:SUBLANES, :] = jnp.zeros((SUBLANES, 3 * DN_W), F32)

    cbuf[SUBLANES:SUBLANES + tm, :] = _dot(h, wdn_ref[...])
    conv = cw_ref[0:1, :] * cbuf[SUBLANES - 3:SUBLANES - 3 + tm, :]
    for i in range(1, DN_CONV):
        conv = conv + cw_ref[i:i + 1, :] * cbuf[SUBLANES - 3 + i:SUBLANES - 3 + i + tm, :]
    cbuf[0:SUBLANES, :] = cbuf[tm:tm + SUBLANES, :]
    act = _silu(conv)
    for hh in range(DN_HEADS):
        for part in range(2):
            lo = part * DN_W + hh * DN_DH
            v = act[:, lo:lo + DN_DH]
            n = v * lax.rsqrt(jnp.sum(v * v, axis=-1, keepdims=True) + EPS)
            if part == 0:
                n = n * (DN_DH ** -0.5)
            dn_ref[:, lo:lo + DN_DH] = n
    dn_ref[:, 2 * DN_W:] = act[:, 2 * DN_W:]


def _inproj(x2d, g_mix, wsb, wdn, wba, wz, wg, conv_w, tm, tiles_per_seq):
    t, d = x2d.shape
    full = lambda a: pl.BlockSpec(a.shape, lambda i: (0, 0))
    row = lambda w: pl.BlockSpec((tm, w), lambda i: (i, 0))
    return pl.pallas_call(
        functools.partial(_inproj_kernel, tm=tm, tiles_per_seq=tiles_per_seq),
        grid=(t // tm,),
        in_specs=[row(d), full(g_mix), full(wsb), full(wdn), full(wba), full(wz), full(wg),
                  full(conv_w)],
        scratch_shapes=[pltpu.VMEM((tm + SUBLANES, 3 * DN_W), F32)],
        out_specs=[row(3 * SB_W), row(3 * DN_W), row(LANES), row(DN_W), row(2 * d)],
        out_shape=[
            jax.ShapeDtypeStruct((t, 3 * SB_W), BF16),
            jax.ShapeDtypeStruct((t, 3 * DN_W), F32),
            jax.ShapeDtypeStruct((t, LANES), F32),
            jax.ShapeDtypeStruct((t, DN_W), BF16),
            jax.ShapeDtypeStruct((t, 2 * d), BF16),
        ],
        compiler_params=pltpu.CompilerParams(
            dimension_semantics=("arbitrary",), vmem_limit_bytes=VMEM_LIMIT),
        name="inproj",
    )(x2d, g_mix, wsb, wdn, wba, wz, wg, conv_w)


def _deltanet_kernel(qkv, ba_ref, z_ref, alog_ref, dtb_ref, ng_ref, o_ref,
                     st0, st1, st2, st3, *, tb):
    t_idx = pl.program_id(1)
    n_chunks = tb // CHUNK
    states = (st0, st1, st2, st3)

    @pl.when(t_idx == 0)
    def _():
        for st in states:
            st[...] = jnp.zeros_like(st)

    ba = ba_ref[...]
    beta = _sigmoid(ba)
    g_all = -jnp.exp(alog_ref[...]) * _softplus(ba + dtb_ref[...])
    r_i = lax.broadcasted_iota(I32, (tb, tb), 0)
    c_i = lax.broadcasted_iota(I32, (tb, tb), 1)
    tri = ((r_i // CHUNK == c_i // CHUNK) & (c_i <= r_i)).astype(BF16)
    g3 = _split3(g_all)
    gc = _dot(tri, g3[0]) + (_dot(tri, g3[1]) + _dot(tri, g3[2]))
    gct = gc.T

    ci = lax.broadcasted_iota(I32, (CHUNK, CHUNK), 0)
    cj = lax.broadcasted_iota(I32, (CHUNK, CHUNK), 1)
    incl = cj <= ci
    strict = cj < ci
    eye = (ci == cj).astype(F32)

    units = [(c, h) for c in range(n_chunks) for h in range(DN_HEADS)]
    q_, k_, v_, gcol_, bcol_, glast_, gamma_ = {}, {}, {}, {}, {}, {}, {}
    for un in units:
        c, h = un
        r = slice(c * CHUNK, (c + 1) * CHUNK)
        q_[un] = qkv[r, h * DN_DH:(h + 1) * DN_DH]
        k_[un] = qkv[r, DN_W + h * DN_DH:DN_W + (h + 1) * DN_DH]
        v_[un] = qkv[r, 2 * DN_W + h * DN_DH:2 * DN_W + (h + 1) * DN_DH]
        gcol_[un] = gc[r, DN_HEADS + h:DN_HEADS + h + 1]
        g_row = gct[DN_HEADS + h:DN_HEADS + h + 1, r]
        bcol_[un] = beta[r, h:h + 1]
        glast_[un] = gc[(c + 1) * CHUNK - 1:(c + 1) * CHUNK, DN_HEADS + h:DN_HEADS + h + 1]
        gamma_[un] = jnp.exp(jnp.where(incl, gcol_[un] - g_row, -jnp.inf))

    qk_kk = {}
    for un in units:
        qb, kb = q_[un].astype(BF16), k_[un].astype(BF16)
        qk_kk[un] = _dot_nt(jnp.concatenate([qb, kb], axis=0), kb)
    a_ = {un: jnp.where(strict, bcol_[un] * qk_kk[un][CHUNK:] * gamma_[un], 0.0) for un in units}

    apow = {un: _mm3(a_[un], a_[un]) for un in units}
    inv = {un: (eye - a_[un]) + _mm3(eye - a_[un], apow[un]) for un in units}
    for _ in range(4):
        apow = {un: _dot(apow[un].astype(BF16), apow[un].astype(BF16)) for un in units}
        inv = {un: inv[un] + _dot(inv[un].astype(BF16), apow[un].astype(BF16)) for un in units}

    eg_ = {un: jnp.exp(gcol_[un]) for un in units}
    rhs_ = {un: jnp.concatenate([v_[un] * bcol_[un], k_[un] * (bcol_[un] * eg_[un])], axis=1)
            for un in units}
    sol_ = {un: rhs_[un] + _mm3(inv[un] - eye, rhs_[un]) for un in units}
    wq_ = {un: jnp.concatenate([sol_[un][:, DN_DH:], q_[un] * eg_[un]], axis=0).astype(BF16)
           for un in units}
    qkm_ = {un: jnp.where(incl, qk_kk[un][:CHUNK] * gamma_[un], 0.0).astype(BF16) for un in units}
    kdec_ = {un: (k_[un] * jnp.exp(glast_[un] - gcol_[un])).astype(BF16) for un in units}

    s_cur = [st[...] for st in states]
    for c in range(n_chunks):
        r = slice(c * CHUNK, (c + 1) * CHUNK)
        hs = [(c, h) for h in range(DN_HEADS)]
        ws_qs = [_dot(wq_[un], s_cur[un[1]].astype(BF16)) for un in hs]
        vn = [(sol_[un][:, :DN_DH] - ws_qs[un[1]][:CHUNK]).astype(BF16) for un in hs]
        o_ = [ws_qs[un[1]][CHUNK:] + _dot(qkm_[un], vn[un[1]]) for un in hs]
        s_cur = [s_cur[un[1]] * jnp.exp(glast_[un]) + _dot_tn(kdec_[un], vn[un[1]]) for un in hs]
        for h in range(DN_HEADS):
            o = o_[h]
            zz = z_ref[r, h * DN_DH:(h + 1) * DN_DH].astype(F32)
            on = o * lax.rsqrt(jnp.mean(o * o, axis=-1, keepdims=True) + EPS) * ng_ref[...]
            o_ref[r, h * DN_DH:(h + 1) * DN_DH] = (on * _silu(zz)).astype(BF16)
    for h in range(DN_HEADS):
        states[h][...] = s_cur[h]


def _deltanet(dn, ba, z, alog_row, dtb_row, norm_g, b, s, tb):
    t = b * s
    nt = s // tb
    row = lambda w: pl.BlockSpec((tb, w), lambda bi, ti: (bi * nt + ti, 0))
    full = lambda a: pl.BlockSpec(a.shape, lambda bi, ti: (0, 0))
    return pl.pallas_call(
        functools.partial(_deltanet_kernel, tb=tb),
        grid=(b, nt),
        in_specs=[row(3 * DN_W), row(LANES), row(DN_W), full(alog_row), full(dtb_row),
                  full(norm_g)],
        out_specs=row(DN_W),
        out_shape=jax.ShapeDtypeStruct((t, DN_W), BF16),
        scratch_shapes=[pltpu.VMEM((DN_DH, DN_DH), F32) for _ in range(DN_HEADS)],
        compiler_params=pltpu.CompilerParams(
            dimension_semantics=("arbitrary", "arbitrary"), vmem_limit_bytes=VMEM_LIMIT),
        name="deltanet",
    )(dn, ba, z, alog_row, dtb_row, norm_g)


def _sbattn_kernel(q_ref, k_ref, v_ref, o_ref, *, tq, pairs):
    qi = pl.program_id(1)
    lane = lax.broadcasted_iota(I32, (tq, LANES), 1)
    first = lane < SB_DH
    zq = jnp.zeros((tq, LANES), BF16)
    q_st = []
    for p in range(pairs):
        q = q_ref[:, p * LANES:(p + 1) * LANES]
        q_st.append(jnp.concatenate([jnp.where(first, q, zq), jnp.where(first, zq, q)], axis=0))
    ri = lax.broadcasted_iota(I32, (2 * tq, tq), 0)
    ci = lax.broadcasted_iota(I32, (2 * tq, tq), 1)
    causal = ci < jnp.where(ri >= tq, ri - tq, ri)
    ui = lax.broadcasted_iota(I32, (tq, tq), 0)
    uj = lax.broadcasted_iota(I32, (tq, tq), 1)
    upper = (ui > uj).astype(BF16)

    def step(kb, carry, masked):
        k0 = pl.multiple_of(kb * tq, tq)
        ps = range(pairs)
        z = [_dot_nt(q_st[p], k_ref[pl.ds(k0, tq), p * LANES:(p + 1) * LANES]) for p in ps]
        sp = [jnp.maximum(z[p], 0.0) + jnp.log(1.0 + jnp.exp(-jnp.abs(z[p]))) for p in ps]
        l = [jnp.where(causal, sp[p], 0.0) for p in ps] if masked else sp
        later = [_dot(l[p].astype(BF16), upper) for p in ps]
        a_cat, v_cat = [], []
        for p in ps:
            a = jnp.exp((z[p] - sp[p]) - later[p] - carry[p][0])
            if masked:
                a = jnp.where(causal, a, 0.0)
            a = a.astype(BF16)
            a_cat.append(jnp.concatenate([a[:tq], a[tq:]], axis=1))
            vblk = v_ref[pl.ds(k0, tq), p * LANES:(p + 1) * LANES]
            v_cat.append(jnp.concatenate(
                [jnp.where(first, vblk, zq), jnp.where(first, zq, vblk)], axis=0))
        return tuple(
            (carry[p][0] + jnp.sum(l[p], axis=1, keepdims=True),
             carry[p][1] + _dot(a_cat[p], v_cat[p])) for p in ps)

    init = tuple((jnp.zeros((2 * tq, 1), F32), jnp.zeros((tq, LANES), F32)) for _ in range(pairs))
    carry = step(qi, init, True)

    def cond(st):
        kb, c = st
        r_min = jnp.min(c[0][0])
        for p in range(1, pairs):
            r_min = jnp.minimum(r_min, jnp.min(c[p][0]))
        return (kb >= 0) & (r_min < F32_EXP_ZERO)

    def body(st):
        kb, c = st
        return kb - 1, step(kb, c, False)

    _, carry = lax.while_loop(cond, body, (qi - 1, carry))
    for p in range(pairs):
        o_ref[:, p * LANES:(p + 1) * LANES] = carry[p][1].astype(BF16)


def _sbattn(sb, b, s, tq):
    t = b * s
    nq = s // tq
    pairs = SB_HEADS // 2
    return pl.pallas_call(
        functools.partial(_sbattn_kernel, tq=tq, pairs=pairs),
        grid=(b, nq),
        in_specs=[
            pl.BlockSpec((tq, SB_W), lambda bi, qi: (bi * nq + qi, 0)),
            pl.BlockSpec((s, SB_W), lambda bi, qi: (bi, 1)),
            pl.BlockSpec((s, SB_W), lambda bi, qi: (bi, 2)),
        ],
        out_specs=pl.BlockSpec((tq, SB_W), lambda bi, qi: (bi * nq + qi, 0)),
        out_shape=jax.ShapeDtypeStruct((t, SB_W), BF16),
        compiler_params=pltpu.CompilerParams(
            dimension_semantics=("arbitrary", "arbitrary"), vmem_limit_bytes=VMEM_LIMIT),
        name="sbattn",
    )(sb, sb, sb)


def _memkv_kernel(mem_ref, g_ref, wkt_ref, wv_ref, kt_ref, v_ref):
    m = mem_ref[0]
    r = lax.rsqrt(jnp.mean(m * m, axis=-1, keepdims=True) + EPS)
    mn = (m * r * g_ref[...]).astype(BF16)
    kt = _dot_nt(wkt_ref[...], mn) * (XA_DH ** -0.5)
    v = _dot(mn, wv_ref[...])
    feat_r = lax.broadcasted_iota(I32, kt.shape, 0) // XA_DH
    feat_c = lax.broadcasted_iota(I32, v.shape, 1) // XA_DH
    for h in range(XA_HEADS):
        kt_ref[0, h] = jnp.where(feat_r == h, kt, 0.0).astype(BF16)
        v_ref[0, h] = jnp.where(feat_c == h, v, 0.0).astype(BF16)


def _memkv(mem, g_mem, wkt, wv):
    b, m, d = mem.shape
    return pl.pallas_call(
        _memkv_kernel,
        grid=(b,),
        in_specs=[
            pl.BlockSpec((1, m, d), lambda i: (i, 0, 0)),
            pl.BlockSpec(g_mem.shape, lambda i: (0, 0)),
            pl.BlockSpec(wkt.shape, lambda i: (0, 0)),
            pl.BlockSpec(wv.shape, lambda i: (0, 0)),
        ],
        out_specs=[
            pl.BlockSpec((1, XA_HEADS, XA_W, m), lambda i: (i, 0, 0, 0)),
            pl.BlockSpec((1, XA_HEADS, m, XA_W), lambda i: (i, 0, 0, 0)),
        ],
        out_shape=[
            jax.ShapeDtypeStruct((b, XA_HEADS, XA_W, m), BF16),
            jax.ShapeDtypeStruct((b, XA_HEADS, m, XA_W), BF16),
        ],
        compiler_params=pltpu.CompilerParams(dimension_semantics=("arbitrary",)),
        name="memkv",
    )(mem, g_mem, wkt, wv)


def _mix_kernel(x_ref, osb_ref, odn_ref, gt_ref, wo_ref, gx_ref, wq_ref, kt_ref, v_ref,
                wxo_ref, gf_ref, wr_hi_ref, wr_lo_ref, br_ref,
                x2_ref, hf_ref, route_ref, cnt_ref, cnt_s, *, tm):
    d = x_ref.shape[1]
    hm = tm // 2
    halves = (slice(0, hm), slice(hm, tm))
    ya = [_dot(osb_ref[r, :], wo_ref[0:SB_W, :]) for r in halves]
    yb = [_dot(odn_ref[r, :], wo_ref[SB_W:, :]) for r in halves]
    x1, hn = [], []
    for j, r in enumerate(halves):
        ga = _sigmoid(gt_ref[r, 0:d]).astype(F32)
        gb = _sigmoid(gt_ref[r, d:]).astype(F32)
        xv = x_ref[r, :] + (ga * ya[j] + gb * yb[j])
        x1.append(xv)
        r1 = lax.rsqrt(jnp.mean(xv * xv, axis=-1, keepdims=True) + EPS)
        hn.append((xv * r1 * gx_ref[...]).astype(BF16))

    q = [_dot(hn[j], wq_ref[...]).astype(BF16) for j in range(2)]
    sc = [[_dot(q[j], kt_ref[0, h]) for h in range(XA_HEADS)] for j in range(2)]
    pr = []
    for j in range(2):
        row = []
        for h in range(XA_HEADS):
            e = jnp.exp(sc[j][h] - jnp.max(sc[j][h], axis=-1, keepdims=True))
            row.append((e / jnp.sum(e, axis=-1, keepdims=True)).astype(BF16))
        pr.append(row)
    o = []
    for j in range(2):
        acc = _dot(pr[j][0], v_ref[0, 0])
        for h in range(1, XA_HEADS):
            acc = acc + _dot(pr[j][h], v_ref[0, h])
        o.append(acc.astype(BF16))
    ca = [_dot(o[j], wxo_ref[...]) for j in range(2)]

    hfs = []
    for j, r in enumerate(halves):
        x2 = x1[j] + ca[j]
        x2_ref[r, :] = x2
        r2 = lax.rsqrt(jnp.mean(x2 * x2, axis=-1, keepdims=True) + EPS)
        hf = x2 * r2 * gf_ref[...]
        for s8 in range(SUBLANES):
            hf_ref[pl.ds(j * hm * SUBLANES + s8, hm, stride=SUBLANES), :] = (
                hf[:, s8 * LANES:(s8 + 1) * LANES])
        hfs.append(_split2(hf))

    logits = jnp.concatenate(
        [_dot(hfs[j][0], wr_hi_ref[...]) + (_dot(hfs[j][1], wr_hi_ref[...])
                                            + _dot(hfs[j][0], wr_lo_ref[...]))
         for j in range(2)], axis=0) + br_ref[...]
    lane = lax.broadcasted_iota(I32, (tm, LANES), 1)
    neg = -jnp.inf
    big = jnp.int32(LANES)
    gl = jnp.where(lane < N_GROUPS, logits, neg)
    gmax = jnp.max(gl, axis=-1, keepdims=True)
    g_sel = jnp.min(jnp.where(gl == gmax, lane, big), axis=-1, keepdims=True)
    p_g = 1.0 / jnp.sum(jnp.exp(gl - gmax), axis=-1, keepdims=True)
    lo = N_GROUPS + g_sel * EPG
    el = jnp.where((lane >= lo) & (lane < lo + EPG), logits, neg)
    v1 = jnp.max(el, axis=-1, keepdims=True)
    i1 = jnp.min(jnp.where(el == v1, lane, big), axis=-1, keepdims=True)
    el2 = jnp.where(lane == i1, neg, el)
    v2 = jnp.max(el2, axis=-1, keepdims=True)
    i2 = jnp.min(jnp.where(el2 == v2, lane, big), axis=-1, keepdims=True)
    e21 = jnp.exp(v2 - v1)
    w1 = p_g / (1.0 + e21)
    w2 = p_g * e21 / (1.0 + e21)

    @pl.when(pl.program_id(0) == 0)
    def _():
        cnt_s[...] = jnp.zeros_like(cnt_s)

    id1 = i1 - N_GROUPS
    id2 = i2 - N_GROUPS
    hit1 = lane == id1
    hit2 = lane == id2
    member = (hit1 | hit2).astype(BF16)
    ti = lax.broadcasted_iota(I32, (tm, tm), 0)
    tj = lax.broadcasted_iota(I32, (tm, tm), 1)
    before = _dot((tj < ti).astype(BF16), member) + cnt_s[...]
    rank1 = jnp.sum(jnp.where(hit1, before, 0.0), axis=-1, keepdims=True)
    rank2 = jnp.sum(jnp.where(hit2, before, 0.0), axis=-1, keepdims=True)
    cnt_new = cnt_s[...] + jnp.sum(member.astype(F32), axis=0, keepdims=True)
    cnt_s[...] = cnt_new
    cnt_ref[...] = cnt_new

    route = jnp.where(lane == 0, id1.astype(F32),
            jnp.where(lane == 1, id2.astype(F32),
            jnp.where(lane == 2, w1,
            jnp.where(lane == 3, w2,
            jnp.where(lane == 4, rank1, jnp.where(lane == 5, rank2, 0.0))))))
    route_ref[...] = route


def _mix(x2d, osb, odn, gates, wo, gx, wq, kt, v, wxo, gf, wr_hi, wr_lo, br, s, tm):
    t, d = x2d.shape
    nb = s // tm
    row = lambda w: pl.BlockSpec((tm, w), lambda i: (i, 0))
    full = lambda a: pl.BlockSpec(a.shape, lambda i: (0,) * a.ndim)
    per_b = lambda a: pl.BlockSpec((1,) + a.shape[1:], lambda i: (i // nb, 0, 0, 0))
    return pl.pallas_call(
        functools.partial(_mix_kernel, tm=tm),
        grid=(t // tm,),
        in_specs=[row(d), row(SB_W), row(DN_W), row(2 * d), full(wo), full(gx), full(wq),
                  per_b(kt), per_b(v), full(wxo), full(gf), full(wr_hi), full(wr_lo), full(br)],
        out_specs=[row(d), pl.BlockSpec((tm * SUBLANES, LANES), lambda i: (i, 0)), row(LANES),
                   pl.BlockSpec((1, LANES), lambda i: (0, 0))],
        out_shape=[
            jax.ShapeDtypeStruct((t, d), F32),
            jax.ShapeDtypeStruct((t * SUBLANES, LANES), F32),
            jax.ShapeDtypeStruct((t, LANES), F32),
            jax.ShapeDtypeStruct((1, LANES), F32),
        ],
        scratch_shapes=[pltpu.VMEM((1, LANES), F32)],
        compiler_params=pltpu.CompilerParams(
            dimension_semantics=("arbitrary",), vmem_limit_bytes=VMEM_LIMIT),
        name="mix",
    )(x2d, osb, odn, gates, wo, gx, wq, kt, v, wxo, gf, wr_hi, wr_lo, br)


def _row_copy(src_hbm, dst, src_row, dst_row, sem):
    return pltpu.make_async_copy(
        src_hbm.at[pl.ds(src_row * SUBLANES, SUBLANES), :],
        dst.at[pl.ds(dst_row * SUBLANES, SUBLANES), :], sem)


def _gather_ring(i, n_valid, idx_hbm, idx_s, row_of, src_hbm, bufs, sem_i, sem_g, rows, consume):
    def idx_copies(tile, sl):
        return [pltpu.make_async_copy(h.at[tile], s, sem_i.at[sl])
                for h, s in zip(idx_hbm, idx_s[sl])]

    def idx_start(tile, sl):
        for cp in idx_copies(tile, sl):
            cp.start()

    def idx_wait(tile, sl):
        for cp in idx_copies(tile, sl):
            cp.wait()

    def wait_rows(sl):
        pltpu.make_async_copy(src_hbm.at[pl.ds(0, rows * SUBLANES), :], bufs[sl],
                              sem_g.at[sl]).wait()

    @pl.when((i == 0) & (n_valid > 0))
    def _():
        idx_start(0, 0)
        idx_wait(0, 0)

        def body(r, c):
            _row_copy(src_hbm, bufs[0], row_of(idx_s[0], r), r, sem_g.at[0]).start()
            return c

        lax.fori_loop(0, rows, body, 0, unroll=8)

        @pl.when(n_valid > 1)
        def _():
            idx_start(1, 1)

    parity = lax.rem(i, 2)
    for sl in range(2):
        nx = 1 - sl

        @pl.when((parity == sl) & (i + 1 < n_valid))
        def _(sl=sl, nx=nx):
            idx_wait(i + 1, nx)

            @pl.when(i + 2 < n_valid)
            def _():
                idx_start(i + 2, sl)

            wait_rows(sl)
            for r in range(rows):
                _row_copy(src_hbm, bufs[nx], row_of(idx_s[nx], r), r,
                          sem_g.at[nx]).start(priority=r % 2)
            consume(bufs[sl])

        @pl.when((parity == sl) & (i + 1 == n_valid))
        def _(sl=sl):
            wait_rows(sl)
            consume(bufs[sl])


def _experts_kernel(tile_e_ref, nused_ref, tok_hbm, hf_hbm, wg_ref, wu_ref, wd_ref, y_ref,
                    tok_s0, tok_s1, xg0, xg1, wgu_s, wd_s, sem_i, sem_g, *, tm):
    i = pl.program_id(0)
    n_used = nused_ref[0]

    @pl.when(i >= n_used)
    def _():
        y_ref[...] = jnp.zeros_like(y_ref)

    new_expert = (i == 0) | (tile_e_ref[i] != tile_e_ref[jnp.maximum(i - 1, 0)])

    @pl.when((i < n_used) & new_expert)
    def _():
        wgu_s[:, :D_EXPERT] = wg_ref[0].astype(BF16)
        wgu_s[:, D_EXPERT:] = wu_ref[0].astype(BF16)
        wd_s[...] = wd_ref[0].astype(BF16)

    def consume(xs):
        x = jnp.concatenate(
            [xs[pl.ds(s8, tm, stride=SUBLANES), :] for s8 in range(SUBLANES)], axis=1).astype(BF16)
        gu = _dot(x, wgu_s[...])
        hmid = (_silu(gu[:, :D_EXPERT]) * gu[:, D_EXPERT:]).astype(BF16)
        y = _dot(hmid, wd_s[...])
        for s8 in range(SUBLANES):
            y_ref[pl.ds(s8, tm, stride=SUBLANES), :] = y[:, s8 * LANES:(s8 + 1) * LANES]

    _gather_ring(i, n_used, (tok_hbm,), ((tok_s0,), (tok_s1,)), lambda refs, r: refs[0][r],
                 hf_hbm, (xg0, xg1), sem_i, sem_g, tm, consume)


def _experts(tile_e, n_used, row_tok, hf3, wg, wu, wd, tm):
    n_tiles = row_tok.shape[0]
    d, de = wg.shape[1], wg.shape[2]
    per_expert = lambda a: pl.BlockSpec((1,) + a.shape[1:], lambda i, te, nu: (te[i], 0, 0))
    grid_spec = pltpu.PrefetchScalarGridSpec(
        num_scalar_prefetch=2,
        grid=(n_tiles,),
        in_specs=[
            pl.BlockSpec(memory_space=pl.ANY),
            pl.BlockSpec(memory_space=pl.ANY),
            per_expert(wg), per_expert(wu), per_expert(wd),
        ],
        out_specs=pl.BlockSpec((tm * SUBLANES, LANES), lambda i, te, nu: (i, 0)),
        scratch_shapes=[
            pltpu.SMEM((tm,), I32),
            pltpu.SMEM((tm,), I32),
            pltpu.VMEM((tm * SUBLANES, LANES), F32),
            pltpu.VMEM((tm * SUBLANES, LANES), F32),
            pltpu.VMEM((d, 2 * de), BF16),
            pltpu.VMEM((de, d), BF16),
            pltpu.SemaphoreType.DMA((2,)),
            pltpu.SemaphoreType.DMA((2,)),
        ],
    )
    return pl.pallas_call(
        functools.partial(_experts_kernel, tm=tm),
        grid_spec=grid_spec,
        out_shape=jax.ShapeDtypeStruct((n_tiles * tm * SUBLANES, LANES), F32),
        compiler_params=pltpu.CompilerParams(
            dimension_semantics=("arbitrary",), vmem_limit_bytes=VMEM_LIMIT),
        name="experts",
    )(tile_e, n_used, row_tok, hf3, wg, wu, wd)


def _combine_kernel(pstart_ref, e_hbm, rank_hbm, y_hbm, x2_ref, route_ref, gfin_ref, o_ref,
                    e_s0, rank_s0, e_s1, rank_s1, yg0, yg1, sem_i, sem_g, *, tm):
    def consume(ys):
        route = route_ref[...]
        w1 = route[:, 2:3]
        w2 = route[:, 3:4]
        pieces = []
        ss = jnp.zeros((tm, 1), F32)
        for s8 in range(SUBLANES):
            y1 = ys[pl.ds(s8, tm, stride=TOP_K * SUBLANES), :]
            y2 = ys[pl.ds(SUBLANES + s8, tm, stride=TOP_K * SUBLANES), :]
            xo = x2_ref[:, s8 * LANES:(s8 + 1) * LANES] + (y1 * w1 + y2 * w2)
            ss = ss + jnp.sum(xo * xo, axis=-1, keepdims=True)
            pieces.append(xo)
        d = x2_ref.shape[1]
        r = lax.rsqrt(ss * (1.0 / d) + EPS)
        for s8 in range(SUBLANES):
            o_ref[:, s8 * LANES:(s8 + 1) * LANES] = (
                pieces[s8] * r * gfin_ref[:, s8 * LANES:(s8 + 1) * LANES])

    _gather_ring(pl.program_id(0), pl.num_programs(0), (e_hbm, rank_hbm),
                 ((e_s0, rank_s0), (e_s1, rank_s1)),
                 lambda refs, r: pstart_ref[refs[0][r]] + refs[1][r],
                 y_hbm, (yg0, yg1), sem_i, sem_g, TOP_K * tm, consume)


def _combine(p_start, slot_e, slot_rank, ybuf, x2, route, g_final, tm):
    t, d = x2.shape
    row = lambda w: pl.BlockSpec((tm, w), lambda i, ps: (i, 0))
    grid_spec = pltpu.PrefetchScalarGridSpec(
        num_scalar_prefetch=1,
        grid=(t // tm,),
        in_specs=[
            pl.BlockSpec(memory_space=pl.ANY),
            pl.BlockSpec(memory_space=pl.ANY),
            pl.BlockSpec(memory_space=pl.ANY),
            row(d), row(LANES),
            pl.BlockSpec(g_final.shape, lambda i, ps: (0, 0)),
        ],
        out_specs=row(d),
        scratch_shapes=[pltpu.SMEM((TOP_K * tm,), I32) for _ in range(4)] + [
            pltpu.VMEM((TOP_K * tm * SUBLANES, LANES), F32),
            pltpu.VMEM((TOP_K * tm * SUBLANES, LANES), F32),
            pltpu.SemaphoreType.DMA((2,)),
            pltpu.SemaphoreType.DMA((2,)),
        ],
    )
    return pl.pallas_call(
        functools.partial(_combine_kernel, tm=tm),
        grid_spec=grid_spec,
        out_shape=jax.ShapeDtypeStruct((t, d), F32),
        compiler_params=pltpu.CompilerParams(
            dimension_semantics=("arbitrary",), vmem_limit_bytes=VMEM_LIMIT),
        name="combine",
    )(p_start, slot_e, slot_rank, ybuf, x2, route, g_final)


def _plan_dispatch(route, counts_row, moe_tm):
    t = route.shape[0]
    n = t * TOP_K
    n_tiles = n // moe_tm + N_EXPERTS
    e = route[:, 0:TOP_K].astype(I32)
    rank = route[:, 4:4 + TOP_K].astype(I32)
    counts = counts_row[0, :N_EXPERTS].astype(I32)
    padded = (counts + moe_tm - 1) // moe_tm * moe_tm
    p_end = jnp.cumsum(padded)
    p_start = (p_end - padded).astype(I32)
    start = jnp.cumsum(counts) - counts
    tile_start = jnp.arange(n_tiles, dtype=I32) * moe_tm
    tile_e = jnp.minimum(jnp.sum(p_end[None, :] <= tile_start[:, None], axis=1), N_EXPERTS - 1)
    tile_e = tile_e.astype(I32)
    n_used = (p_end[-1] // moe_tm).astype(I32).reshape(1)
    order = jnp.argsort(e.reshape(n), stable=True).astype(I32)
    off = jnp.arange(moe_tm, dtype=I32)[None, :] + (tile_start - p_start[tile_e])[:, None]
    valid = off < counts[tile_e][:, None]
    src = jnp.clip(start[tile_e][:, None] + off, 0, n - 1)
    row_tok = jnp.where(valid, order[src] // TOP_K, 0)
    return tile_e, n_used, row_tok, p_start, e, rank


def _lane_row(vals, offset):
    row = jnp.zeros((1, LANES), F32)
    return row.at[0, offset:offset + vals.shape[0]].set(vals.astype(F32))


def _forward(x, mem, g_mix, w_in, dn_conv_w, dn_a_log, dn_dt_bias, dn_norm_g, w_out,
             g_xattn, g_mem, xa_wq, xa_wkv, xa_wo, g_ffn, w_grp, b_grp, w_exp, b_exp,
             we_gate, we_up, we_down, g_final, *, tm_in, tb_dn, tq_sb, tm_mix, moe_tm, tm_out):
    b, s, d = x.shape
    t = b * s
    x2d = x.reshape(t, d)
    assert g_mix.shape[0] == 1, "the final RMSNorm is fused into the (single) layer's combine step"
    for l in range(1):
        w = w_in[l]
        o0 = 3 * SB_W
        o1 = o0 + 3 * DN_W
        o2 = o1 + 2 * DN_HEADS
        o3 = o2 + DN_W
        wsb = jnp.concatenate([w[:, :SB_W] * (SB_DH ** -0.5), w[:, SB_W:o0]], axis=1).astype(BF16)
        wdn = w[:, o0:o1].astype(BF16)
        wba = jnp.pad(w[:, o1:o2], ((0, 0), (0, LANES - 2 * DN_HEADS))).astype(BF16)
        wz = w[:, o2:o3].astype(BF16)
        wg = w[:, o3:].astype(BF16)
        sb, dn, ba, z, gates = _inproj(x2d, g_mix[l][None, :], wsb, wdn, wba, wz, wg,
                                       dn_conv_w[l], tm_in, s // tm_in)

        o_sb = _sbattn(sb, b, s, tq_sb)
        o_dn = _deltanet(dn, ba, z, _lane_row(dn_a_log[l], DN_HEADS),
                         _lane_row(dn_dt_bias[l], DN_HEADS), dn_norm_g[l][None, :], b, s, tb_dn)

        wkv = xa_wkv[l]
        kt, v = _memkv(mem, g_mem[l][None, :], wkv[:, :XA_W].T.astype(BF16),
                       wkv[:, XA_W:].astype(BF16))

        w_route = jnp.pad(jnp.concatenate([w_grp[l], w_exp[l]], axis=1),
                          ((0, 0), (0, LANES - N_GROUPS - N_EXPERTS)))
        wr_hi = w_route.astype(BF16)
        wr_lo = (w_route - wr_hi.astype(F32)).astype(BF16)
        b_route = _lane_row(jnp.concatenate([b_grp[l], b_exp[l]]), 0)
        x2, hf3, route, counts = _mix(x2d, o_sb, o_dn, gates, w_out[l].astype(BF16),
                                      g_xattn[l][None, :], xa_wq[l].astype(BF16), kt, v,
                                      xa_wo[l].astype(BF16), g_ffn[l][None, :], wr_hi, wr_lo,
                                      b_route, s, tm_mix)

        tile_e, n_used, row_tok, p_start, slot_e, slot_rank = _plan_dispatch(route, counts, moe_tm)
        ybuf = _experts(tile_e, n_used, row_tok, hf3, we_gate[l], we_up[l], we_down[l], moe_tm)
        per_tile = (t // tm_out, TOP_K * tm_out)
        x2d = _combine(p_start, slot_e.reshape(per_tile), slot_rank.reshape(per_tile), ybuf, x2,
                       route, g_final[None, :], tm_out)
    return x2d.reshape(b, s, d)


def kernel(x, mem, g_mix, w_in, dn_conv_w, dn_a_log, dn_dt_bias, dn_norm_g, w_out, g_xattn, g_mem,
           xa_wq, xa_wkv, xa_wo, g_ffn, w_grp, b_grp, w_exp, b_exp, we_gate, we_up, we_down, g_final):
    assert g_mix.shape[0] == 1, "single-layer block"
    return _forward(x, mem, g_mix, w_in, dn_conv_w, dn_a_log, dn_dt_bias, dn_norm_g, w_out,
                    g_xattn, g_mem, xa_wq, xa_wkv, xa_wo, g_ffn, w_grp, b_grp, w_exp, b_exp,
                    we_gate, we_up, we_down, g_final,
                    tm_in=256, tb_dn=256, tq_sb=128, tm_mix=512, moe_tm=512, tm_out=256)
```

```python
import functools

import jax
import jax.numpy as jnp
from jax import lax
from jax.experimental import pallas as pl
from jax.experimental.pallas import tpu as pltpu

F32 = jnp.float32
BF16 = jnp.bfloat16
I32 = jnp.int32

EPS = 1e-6
LANES = 128
SUBLANES = 8

SB_HEADS, SB_DH = 8, 64
SB_W = SB_HEADS * SB_DH
DN_HEADS, DN_DH = 4, 128
DN_W = DN_HEADS * DN_DH
DN_CONV = 4
CHUNK = 64
XA_HEADS, XA_DH = 4, 64
XA_W = XA_HEADS * XA_DH
N_GROUPS, EPG = 4, 8
N_EXPERTS = N_GROUPS * EPG
TOP_K = 2
D_EXPERT = 512

F32_EXP_ZERO = 104.0

VMEM_LIMIT = 56 * 1024 * 1024


def _dot(a, b):
    return jnp.dot(a, b, preferred_element_type=F32)


def _dot_nt(a, b):
    return lax.dot_general(a, b, (((1,), (1,)), ((), ())), preferred_element_type=F32)


def _dot_tn(a, b):
    return lax.dot_general(a, b, (((0,), (0,)), ((), ())), preferred_element_type=F32)


def _split2(a):
    hi = a.astype(BF16)
    lo = (a - hi.astype(F32)).astype(BF16)
    return hi, lo


def _split3(a):
    hi = a.astype(BF16)
    r = a - hi.astype(F32)
    mid = r.astype(BF16)
    lo = (r - mid.astype(F32)).astype(BF16)
    return hi, mid, lo


def _mm3(a, b):
    ah, al = _split2(a)
    bh, bl = _split2(b)
    return _dot(ah, bh) + (_dot(ah, bl) + _dot(al, bh))


def _softplus(z):
    return jnp.maximum(z, 0.0) + jnp.log1p(jnp.exp(-jnp.abs(z)))


def _sigmoid(z):
    return 1.0 / (1.0 + jnp.exp(-z))


def _silu(z):
    return z * _sigmoid(z)


def _inproj_kernel(x_ref, g_ref, wsb_ref, wdn_ref, wba_ref, wz_ref, wg_ref, cw_ref,
                   sb_ref, dn_ref, ba_ref, z_ref, gt_ref, cbuf, *, tm, tiles_per_seq):
    x = x_ref[...]
    r = lax.rsqrt(jnp.mean(x * x, axis=-1, keepdims=True) + EPS)
    h = (x * r * g_ref[...]).astype(BF16)
    sb_ref[...] = _dot(h, wsb_ref[...]).astype(BF16)
    ba_ref[...] = _dot(h, wba_ref[...])
    z_ref[...] = _dot(h, wz_ref[...]).astype(BF16)
    gt_ref[...] = _dot(h, wg_ref[...]).astype(BF16)

    @pl.when(pl.program_id(0) % tiles_per_seq == 0)
    def _():
        cbuf[0:SUBLANES, :] = jnp.zeros((SUBLANES, 3 * DN_W), F32)

    cbuf[SUBLANES:SUBLANES + tm, :] = _dot(h, wdn_ref[...])
    conv = cw_ref[0:1, :] * cbuf[SUBLANES - 3:SUBLANES - 3 + tm, :]
    for i in range(1, DN_CONV):
        conv = conv + cw_ref[i:i + 1, :] * cbuf[SUBLANES - 3 + i:SUBLANES - 3 + i + tm, :]
    cbuf[0:SUBLANES, :] = cbuf[tm:tm + SUBLANES, :]
    act = _silu(conv)
    for hh in range(DN_HEADS):
        for part in range(2):
            lo = part * DN_W + hh * DN_DH
            v = act[:, lo:lo + DN_DH]
            n = v * lax.rsqrt(jnp.sum(v * v, axis=-1, keepdims=True) + EPS)
            if part == 0:
                n = n * (DN_DH ** -0.5)
            dn_ref[:, lo:lo + DN_DH] = n
    dn_ref[:, 2 * DN_W:] = act[:, 2 * DN_W:]


def _inproj(x2d, g_mix, wsb, wdn, wba, wz, wg, conv_w, tm, tiles_per_seq):
    t, d = x2d.shape
    full = lambda a: pl.BlockSpec(a.shape, lambda i: (0, 0), pipeline_mode=pl.Buffered(1))
    row = lambda w: pl.BlockSpec((tm, w), lambda i: (i, 0))
    return pl.pallas_call(
        functools.partial(_inproj_kernel, tm=tm, tiles_per_seq=tiles_per_seq),
        grid=(t // tm,),
        in_specs=[row(d), full(g_mix), full(wsb), full(wdn), full(wba), full(wz), full(wg),
                  full(conv_w)],
        scratch_shapes=[pltpu.VMEM((tm + SUBLANES, 3 * DN_W), F32)],
        out_specs=[row(3 * SB_W), row(3 * DN_W), row(LANES), row(DN_W), row(2 * d)],
        out_shape=[
            jax.ShapeDtypeStruct((t, 3 * SB_W), BF16),
            jax.ShapeDtypeStruct((t, 3 * DN_W), F32),
            jax.ShapeDtypeStruct((t, LANES), F32),
            jax.ShapeDtypeStruct((t, DN_W), BF16),
            jax.ShapeDtypeStruct((t, 2 * d), BF16),
        ],
        compiler_params=pltpu.CompilerParams(
            dimension_semantics=("arbitrary",), vmem_limit_bytes=VMEM_LIMIT),
        name="inproj",
    )(x2d, g_mix, wsb, wdn, wba, wz, wg, conv_w)


def _deltanet_kernel(qkv, ba_ref, z_ref, alog_ref, dtb_ref, ng_ref, o_ref,
                     st0, st1, st2, st3, *, tb):
    t_idx = pl.program_id(1)
    n_chunks = tb // CHUNK
    states = (st0, st1, st2, st3)

    @pl.when(t_idx == 0)
    def _():
        for st in states:
            st[...] = jnp.zeros_like(st)

    ba = ba_ref[...]
    beta = _sigmoid(ba)
    g_all = -jnp.exp(alog_ref[...]) * _softplus(ba + dtb_ref[...])
    r_i = lax.broadcasted_iota(I32, (tb, tb), 0)
    c_i = lax.broadcasted_iota(I32, (tb, tb), 1)
    tri = ((r_i // CHUNK == c_i // CHUNK) & (c_i <= r_i)).astype(BF16)
    g3 = _split3(g_all)
    gc = _dot(tri, g3[0]) + (_dot(tri, g3[1]) + _dot(tri, g3[2]))
    gct = gc.T

    ci = lax.broadcasted_iota(I32, (CHUNK, CHUNK), 0)
    cj = lax.broadcasted_iota(I32, (CHUNK, CHUNK), 1)
    incl = cj <= ci
    strict = cj < ci
    eye = (ci == cj).astype(F32)

    units = [(c, h) for c in range(n_chunks) for h in range(DN_HEADS)]
    q_, k_, v_, gcol_, bcol_, glast_, gamma_ = {}, {}, {}, {}, {}, {}, {}
    for un in units:
        c, h = un
        r = slice(c * CHUNK, (c + 1) * CHUNK)
        q_[un] = qkv[r, h * DN_DH:(h + 1) * DN_DH]
        k_[un] = qkv[r, DN_W + h * DN_DH:DN_W + (h + 1) * DN_DH]
        v_[un] = qkv[r, 2 * DN_W + h * DN_DH:2 * DN_W + (h + 1) * DN_DH]
        gcol_[un] = gc[r, DN_HEADS + h:DN_HEADS + h + 1]
        g_row = gct[DN_HEADS + h:DN_HEADS + h + 1, r]
        bcol_[un] = beta[r, h:h + 1]
        glast_[un] = gc[(c + 1) * CHUNK - 1:(c + 1) * CHUNK, DN_HEADS + h:DN_HEADS + h + 1]
        gamma_[un] = jnp.exp(jnp.where(incl, gcol_[un] - g_row, -jnp.inf))

    qk_kk = {}
    for un in units:
        qb, kb = q_[un].astype(BF16), k_[un].astype(BF16)
        qk_kk[un] = _dot_nt(jnp.concatenate([qb, kb], axis=0), kb)
    a_ = {un: jnp.where(strict, bcol_[un] * qk_kk[un][CHUNK:] * gamma_[un], 0.0) for un in units}

    apow = {un: _mm3(a_[un], a_[un]) for un in units}
    inv = {un: (eye - a_[un]) + _mm3(eye - a_[un], apow[un]) for un in units}
    for _ in range(4):
        apow = {un: _dot(apow[un].astype(BF16), apow[un].astype(BF16)) for un in units}
        inv = {un: inv[un] + _dot(inv[un].astype(BF16), apow[un].astype(BF16)) for un in units}

    eg_ = {un: jnp.exp(gcol_[un]) for un in units}
    rhs_ = {un: jnp.concatenate([v_[un] * bcol_[un], k_[un] * (bcol_[un] * eg_[un])], axis=1)
            for un in units}
    sol_ = {un: rhs_[un] + _mm3(inv[un] - eye, rhs_[un]) for un in units}
    wq_ = {un: jnp.concatenate([sol_[un][:, DN_DH:], q_[un] * eg_[un]], axis=0).astype(BF16)
           for un in units}
    qkm_ = {un: jnp.where(incl, qk_kk[un][:CHUNK] * gamma_[un], 0.0).astype(BF16) for un in units}
    kdec_ = {un: (k_[un] * jnp.exp(glast_[un] - gcol_[un])).astype(BF16) for un in units}

    s_cur = [st[...] for st in states]
    for c in range(n_chunks):
        r = slice(c * CHUNK, (c + 1) * CHUNK)
        hs = [(c, h) for h in range(DN_HEADS)]
        ws_qs = [_dot(wq_[un], s_cur[un[1]].astype(BF16)) for un in hs]
        vn = [(sol_[un][:, :DN_DH] - ws_qs[un[1]][:CHUNK]).astype(BF16) for un in hs]
        o_ = [ws_qs[un[1]][CHUNK:] + _dot(qkm_[un], vn[un[1]]) for un in hs]
        s_cur = [s_cur[un[1]] * jnp.exp(glast_[un]) + _dot_tn(kdec_[un], vn[un[1]]) for un in hs]
        for h in range(DN_HEADS):
            o = o_[h]
            zz = z_ref[r, h * DN_DH:(h + 1) * DN_DH].astype(F32)
            on = o * lax.rsqrt(jnp.mean(o * o, axis=-1, keepdims=True) + EPS) * ng_ref[...]
            o_ref[r, h * DN_DH:(h + 1) * DN_DH] = (on * _silu(zz)).astype(BF16)
    for h in range(DN_HEADS):
        states[h][...] = s_cur[h]


def _deltanet(dn, ba, z, alog_row, dtb_row, norm_g, b, s, tb):
    t = b * s
    nt = s // tb
    row = lambda w: pl.BlockSpec((tb, w), lambda bi, ti: (bi * nt + ti, 0))
    full = lambda a: pl.BlockSpec(a.shape, lambda bi, ti: (0, 0))
    return pl.pallas_call(
        functools.partial(_deltanet_kernel, tb=tb),
        grid=(b, nt),
        in_specs=[row(3 * DN_W), row(LANES), row(DN_W), full(alog_row), full(dtb_row),
                  full(norm_g)],
        out_specs=row(DN_W),
        out_shape=jax.ShapeDtypeStruct((t, DN_W), BF16),
        scratch_shapes=[pltpu.VMEM((DN_DH, DN_DH), F32) for _ in range(DN_HEADS)],
        compiler_params=pltpu.CompilerParams(
            dimension_semantics=("arbitrary", "arbitrary"), vmem_limit_bytes=VMEM_LIMIT),
        name="deltanet",
    )(dn, ba, z, alog_row, dtb_row, norm_g)


def _sbattn_kernel(q_ref, k_ref, v_ref, o_ref, *, tq, pairs):
    qi = pl.program_id(1)
    lane = lax.broadcasted_iota(I32, (tq, LANES), 1)
    first = lane < SB_DH
    zq = jnp.zeros((tq, LANES), BF16)
    q_st = []
    for p in range(pairs):
        q = q_ref[:, p * LANES:(p + 1) * LANES]
        q_st.append(jnp.concatenate([jnp.where(first, q, zq), jnp.where(first, zq, q)], axis=0))
    ri = lax.broadcasted_iota(I32, (2 * tq, tq), 0)
    ci = lax.broadcasted_iota(I32, (2 * tq, tq), 1)
    causal = ci < jnp.where(ri >= tq, ri - tq, ri)
    ui = lax.broadcasted_iota(I32, (tq, tq), 0)
    uj = lax.broadcasted_iota(I32, (tq, tq), 1)
    upper = (ui > uj).astype(BF16)

    def step(kb, carry, masked):
        k0 = pl.multiple_of(kb * tq, tq)
        ps = range(pairs)
        z = [_dot_nt(q_st[p], k_ref[pl.ds(k0, tq), p * LANES:(p + 1) * LANES]) for p in ps]
        sp = [jnp.maximum(z[p], 0.0) + jnp.log(1.0 + jnp.exp(-jnp.abs(z[p]))) for p in ps]
        l = [jnp.where(causal, sp[p], 0.0) for p in ps] if masked else sp
        later = [_dot(l[p].astype(BF16), upper) for p in ps]
        a_cat, v_cat = [], []
        for p in ps:
            a = jnp.exp((z[p] - sp[p]) - later[p] - carry[p][0])
            if masked:
                a = jnp.where(causal, a, 0.0)
            a = a.astype(BF16)
            a_cat.append(jnp.concatenate([a[:tq], a[tq:]], axis=1))
            vblk = v_ref[pl.ds(k0, tq), p * LANES:(p + 1) * LANES]
            v_cat.append(jnp.concatenate(
                [jnp.where(first, vblk, zq), jnp.where(first, zq, vblk)], axis=0))
        return tuple(
            (carry[p][0] + jnp.sum(l[p], axis=1, keepdims=True),
             carry[p][1] + _dot(a_cat[p], v_cat[p])) for p in ps)

    init = tuple((jnp.zeros((2 * tq, 1), F32), jnp.zeros((tq, LANES), F32)) for _ in range(pairs))
    carry = step(qi, init, True)

    def cond(st):
        kb, c = st
        r_min = jnp.min(c[0][0])
        for p in range(1, pairs):
            r_min = jnp.minimum(r_min, jnp.min(c[p][0]))
        return (kb >= 0) & (r_min < F32_EXP_ZERO)

    def body(st):
        kb, c = st
        return kb - 1, step(kb, c, False)

    _, carry = lax.while_loop(cond, body, (qi - 1, carry))
    for p in range(pairs):
        o_ref[:, p * LANES:(p + 1) * LANES] = carry[p][1].astype(BF16)


def _sbattn(sb, b, s, tq):
    t = b * s
    nq = s // tq
    pairs = SB_HEADS // 2
    return pl.pallas_call(
        functools.partial(_sbattn_kernel, tq=tq, pairs=pairs),
        grid=(b, nq),
        in_specs=[
            pl.BlockSpec((tq, SB_W), lambda bi, qi: (bi * nq + qi, 0)),
            pl.BlockSpec((s, SB_W), lambda bi, qi: (bi, 1)),
            pl.BlockSpec((s, SB_W), lambda bi, qi: (bi, 2)),
        ],
        out_specs=pl.BlockSpec((tq, SB_W), lambda bi, qi: (bi * nq + qi, 0)),
        out_shape=jax.ShapeDtypeStruct((t, SB_W), BF16),
        compiler_params=pltpu.CompilerParams(
            dimension_semantics=("arbitrary", "arbitrary"), vmem_limit_bytes=VMEM_LIMIT),
        name="sbattn",
    )(sb, sb, sb)


def _memkv_kernel(mem_ref, g_ref, wkt_ref, wv_ref, kt_ref, v_ref):
    m = mem_ref[0]
    r = lax.rsqrt(jnp.mean(m * m, axis=-1, keepdims=True) + EPS)
    mn = (m * r * g_ref[...]).astype(BF16)
    kt = _dot_nt(wkt_ref[...], mn) * (XA_DH ** -0.5)
    v = _dot(mn, wv_ref[...])
    feat_r = lax.broadcasted_iota(I32, kt.shape, 0) // XA_DH
    feat_c = lax.broadcasted_iota(I32, v.shape, 1) // XA_DH
    for h in range(XA_HEADS):
        kt_ref[0, h] = jnp.where(feat_r == h, kt, 0.0).astype(BF16)
        v_ref[0, h] = jnp.where(feat_c == h, v, 0.0).astype(BF16)


def _memkv(mem, g_mem, wkt, wv):
    b, m, d = mem.shape
    return pl.pallas_call(
        _memkv_kernel,
        grid=(b,),
        in_specs=[
            pl.BlockSpec((1, m, d), lambda i: (i, 0, 0)),
            pl.BlockSpec(g_mem.shape, lambda i: (0, 0)),
            pl.BlockSpec(wkt.shape, lambda i: (0, 0)),
            pl.BlockSpec(wv.shape, lambda i: (0, 0)),
        ],
        out_specs=[
            pl.BlockSpec((1, XA_HEADS, XA_W, m), lambda i: (i, 0, 0, 0)),
            pl.BlockSpec((1, XA_HEADS, m, XA_W), lambda i: (i, 0, 0, 0)),
        ],
        out_shape=[
            jax.ShapeDtypeStruct((b, XA_HEADS, XA_W, m), BF16),
            jax.ShapeDtypeStruct((b, XA_HEADS, m, XA_W), BF16),
        ],
        compiler_params=pltpu.CompilerParams(dimension_semantics=("arbitrary",)),
        name="memkv",
    )(mem, g_mem, wkt, wv)


def _mix_kernel(x_ref, osb_ref, odn_ref, gt_ref, wo_ref, gx_ref, wq_ref, kt_ref, v_ref,
                wxo_ref, gf_ref, wr_hi_ref, wr_lo_ref, br_ref,
                x2_ref, hf_ref, route_ref, cnt_ref, cnt_s, *, tm):
    d = x_ref.shape[1]
    hm = tm // 2
    halves = (slice(0, hm), slice(hm, tm))
    ya = [_dot(osb_ref[r, :], wo_ref[0:SB_W, :]) for r in halves]
    yb = [_dot(odn_ref[r, :], wo_ref[SB_W:, :]) for r in halves]
    x1, hn = [], []
    for j, r in enumerate(halves):
        ga = _sigmoid(gt_ref[r, 0:d]).astype(F32)
        gb = _sigmoid(gt_ref[r, d:]).astype(F32)
        xv = x_ref[r, :] + (ga * ya[j] + gb * yb[j])
        x1.append(xv)
        r1 = lax.rsqrt(jnp.mean(xv * xv, axis=-1, keepdims=True) + EPS)
        hn.append((xv * r1 * gx_ref[...]).astype(BF16))

    q = [_dot(hn[j], wq_ref[...]).astype(BF16) for j in range(2)]
    sc = [[_dot(q[j], kt_ref[0, h]) for h in range(XA_HEADS)] for j in range(2)]
    pr = []
    for j in range(2):
        row = []
        for h in range(XA_HEADS):
            e = jnp.exp(sc[j][h] - jnp.max(sc[j][h], axis=-1, keepdims=True))
            row.append((e / jnp.sum(e, axis=-1, keepdims=True)).astype(BF16))
        pr.append(row)
    o = []
    for j in range(2):
        acc = _dot(pr[j][0], v_ref[0, 0])
        for h in range(1, XA_HEADS):
            acc = acc + _dot(pr[j][h], v_ref[0, h])
        o.append(acc.astype(BF16))
    ca = [_dot(o[j], wxo_ref[...]) for j in range(2)]

    hfs = []
    for j, r in enumerate(halves):
        x2 = x1[j] + ca[j]
        x2_ref[r, :] = x2
        r2 = lax.rsqrt(jnp.mean(x2 * x2, axis=-1, keepdims=True) + EPS)
        hf = x2 * r2 * gf_ref[...]
        for s8 in range(SUBLANES):
            hf_ref[pl.ds(j * hm * SUBLANES + s8, hm, stride=SUBLANES), :] = (
                hf[:, s8 * LANES:(s8 + 1) * LANES])
        hfs.append(_split2(hf))

    logits = jnp.concatenate(
        [_dot(hfs[j][0], wr_hi_ref[...]) + (_dot(hfs[j][1], wr_hi_ref[...])
                                            + _dot(hfs[j][0], wr_lo_ref[...]))
         for j in range(2)], axis=0) + br_ref[...]
    lane = lax.broadcasted_iota(I32, (tm, LANES), 1)
    neg = -jnp.inf
    big = jnp.int32(LANES)
    gl = jnp.where(lane < N_GROUPS, logits, neg)
    gmax = jnp.max(gl, axis=-1, keepdims=True)
    g_sel = jnp.min(jnp.where(gl == gmax, lane, big), axis=-1, keepdims=True)
    p_g = 1.0 / jnp.sum(jnp.exp(gl - gmax), axis=-1, keepdims=True)
    lo = N_GROUPS + g_sel * EPG
    el = jnp.where((lane >= lo) & (lane < lo + EPG), logits, neg)
    v1 = jnp.max(el, axis=-1, keepdims=True)
    i1 = jnp.min(jnp.where(el == v1, lane, big), axis=-1, keepdims=True)
    el2 = jnp.where(lane == i1, neg, el)
    v2 = jnp.max(el2, axis=-1, keepdims=True)
    i2 = jnp.min(jnp.where(el2 == v2, lane, big), axis=-1, keepdims=True)
    e21 = jnp.exp(v2 - v1)
    w1 = p_g / (1.0 + e21)
    w2 = p_g * e21 / (1.0 + e21)

    @pl.when(pl.program_id(0) == 0)
    def _():
        cnt_s[...] = jnp.zeros_like(cnt_s)

    id1 = i1 - N_GROUPS
    id2 = i2 - N_GROUPS
    hit1 = lane == id1
    hit2 = lane == id2
    member = (hit1 | hit2).astype(BF16)
    ti = lax.broadcasted_iota(I32, (tm, tm), 0)
    tj = lax.broadcasted_iota(I32, (tm, tm), 1)
    before = _dot((tj < ti).astype(BF16), member) + cnt_s[...]
    rank1 = jnp.sum(jnp.where(hit1, before, 0.0), axis=-1, keepdims=True)
    rank2 = jnp.sum(jnp.where(hit2, before, 0.0), axis=-1, keepdims=True)
    cnt_new = cnt_s[...] + jnp.sum(member.astype(F32), axis=0, keepdims=True)
    cnt_s[...] = cnt_new
    cnt_ref[...] = cnt_new

    route = jnp.where(lane == 0, id1.astype(F32),
            jnp.where(lane == 1, id2.astype(F32),
            jnp.where(lane == 2, w1,
            jnp.where(lane == 3, w2,
            jnp.where(lane == 4, rank1, jnp.where(lane == 5, rank2, 0.0))))))
    route_ref[...] = route


def _mix(x2d, osb, odn, gates, wo, gx, wq, kt, v, wxo, gf, wr_hi, wr_lo, br, s, tm):
    t, d = x2d.shape
    nb = s // tm
    row = lambda w: pl.BlockSpec((tm, w), lambda i: (i, 0))
    full = lambda a: pl.BlockSpec(a.shape, lambda i: (0,) * a.ndim)
    per_b = lambda a: pl.BlockSpec((1,) + a.shape[1:], lambda i: (i // nb, 0, 0, 0))
    return pl.pallas_call(
        functools.partial(_mix_kernel, tm=tm),
        grid=(t // tm,),
        in_specs=[row(d), row(SB_W), row(DN_W), row(2 * d), full(wo), full(gx), full(wq),
                  per_b(kt), per_b(v), full(wxo), full(gf), full(wr_hi), full(wr_lo), full(br)],
        out_specs=[row(d), pl.BlockSpec((tm * SUBLANES, LANES), lambda i: (i, 0)), row(LANES),
                   pl.BlockSpec((1, LANES), lambda i: (0, 0))],
        out_shape=[
            jax.ShapeDtypeStruct((t, d), F32),
            jax.ShapeDtypeStruct((t * SUBLANES, LANES), F32),
            jax.ShapeDtypeStruct((t, LANES), F32),
            jax.ShapeDtypeStruct((1, LANES), F32),
        ],
        scratch_shapes=[pltpu.VMEM((1, LANES), F32)],
        compiler_params=pltpu.CompilerParams(
            dimension_semantics=("arbitrary",), vmem_limit_bytes=VMEM_LIMIT),
        name="mix",
    )(x2d, osb, odn, gates, wo, gx, wq, kt, v, wxo, gf, wr_hi, wr_lo, br)


def _row_copy(src_hbm, dst, src_row, dst_row, sem):
    return pltpu.make_async_copy(
        src_hbm.at[pl.ds(src_row * SUBLANES, SUBLANES), :],
        dst.at[pl.ds(dst_row * SUBLANES, SUBLANES), :], sem)


def _gather_ring(i, n_valid, idx_hbm, idx_s, row_of, src_hbm, bufs, sem_i, sem_g, rows, consume):
    def idx_copies(tile, sl):
        return [pltpu.make_async_copy(h.at[tile], s, sem_i.at[sl])
                for h, s in zip(idx_hbm, idx_s[sl])]

    def idx_start(tile, sl):
        for cp in idx_copies(tile, sl):
            cp.start()

    def idx_wait(tile, sl):
        for cp in idx_copies(tile, sl):
            cp.wait()

    def wait_rows(sl):
        pltpu.make_async_copy(src_hbm.at[pl.ds(0, rows * SUBLANES), :], bufs[sl],
                              sem_g.at[sl]).wait()

    @pl.when((i == 0) & (n_valid > 0))
    def _():
        idx_start(0, 0)
        idx_wait(0, 0)

        def body(r, c):
            _row_copy(src_hbm, bufs[0], row_of(idx_s[0], r), r, sem_g.at[0]).start()
            return c

        lax.fori_loop(0, rows, body, 0, unroll=8)

        @pl.when(n_valid > 1)
        def _():
            idx_start(1, 1)

    parity = lax.rem(i, 2)
    for sl in range(2):
        nx = 1 - sl

        @pl.when((parity == sl) & (i + 1 < n_valid))
        def _(sl=sl, nx=nx):
            idx_wait(i + 1, nx)

            @pl.when(i + 2 < n_valid)
            def _():
                idx_start(i + 2, sl)

            wait_rows(sl)
            for r in range(rows):
                _row_copy(src_hbm, bufs[nx], row_of(idx_s[nx], r), r,
                          sem_g.at[nx]).start(priority=r % 2)
            consume(bufs[sl])

        @pl.when((parity == sl) & (i + 1 == n_valid))
        def _(sl=sl):
            wait_rows(sl)
            consume(bufs[sl])


def _experts_kernel(tile_e_ref, nused_ref, tok_hbm, hf_hbm, wg_ref, wu_ref, wd_ref, y_ref,
                    tok_s0, tok_s1, xg0, xg1, wgu_s, wd_s, sem_i, sem_g, *, tm):
    i = pl.program_id(0)
    n_used = nused_ref[0]

    @pl.when(i >= n_used)
    def _():
        y_ref[...] = jnp.zeros_like(y_ref)

    new_expert = (i == 0) | (tile_e_ref[i] != tile_e_ref[jnp.maximum(i - 1, 0)])

    @pl.when((i < n_used) & new_expert)
    def _():
        wgu_s[:, :D_EXPERT] = wg_ref[0].astype(BF16)
        wgu_s[:, D_EXPERT:] = wu_ref[0].astype(BF16)
        wd_s[...] = wd_ref[0].astype(BF16)

    def consume(xs):
        x = jnp.concatenate(
            [xs[pl.ds(s8, tm, stride=SUBLANES), :] for s8 in range(SUBLANES)], axis=1).astype(BF16)
        gu = _dot(x, wgu_s[...])
        hmid = (_silu(gu[:, :D_EXPERT]) * gu[:, D_EXPERT:]).astype(BF16)
        y = _dot(hmid, wd_s[...])
        for s8 in range(SUBLANES):
            y_ref[pl.ds(s8, tm, stride=SUBLANES), :] = y[:, s8 * LANES:(s8 + 1) * LANES]

    _gather_ring(i, n_used, (tok_hbm,), ((tok_s0,), (tok_s1,)), lambda refs, r: refs[0][r],
                 hf_hbm, (xg0, xg1), sem_i, sem_g, tm, consume)


def _experts(tile_e, n_used, row_tok, hf3, wg, wu, wd, tm):
    n_tiles = row_tok.shape[0]
    d, de = wg.shape[1], wg.shape[2]
    per_expert = lambda a: pl.BlockSpec((1,) + a.shape[1:], lambda i, te, nu: (te[i], 0, 0))
    grid_spec = pltpu.PrefetchScalarGridSpec(
        num_scalar_prefetch=2,
        grid=(n_tiles,),
        in_specs=[
            pl.BlockSpec(memory_space=pl.ANY),
            pl.BlockSpec(memory_space=pl.ANY),
            per_expert(wg), per_expert(wu), per_expert(wd),
        ],
        out_specs=pl.BlockSpec((tm * SUBLANES, LANES), lambda i, te, nu: (i, 0)),
        scratch_shapes=[
            pltpu.SMEM((tm,), I32),
            pltpu.SMEM((tm,), I32),
            pltpu.VMEM((tm * SUBLANES, LANES), F32),
            pltpu.VMEM((tm * SUBLANES, LANES), F32),
            pltpu.VMEM((d, 2 * de), BF16),
            pltpu.VMEM((de, d), BF16),
            pltpu.SemaphoreType.DMA((2,)),
            pltpu.SemaphoreType.DMA((2,)),
        ],
    )
    return pl.pallas_call(
        functools.partial(_experts_kernel, tm=tm),
        grid_spec=grid_spec,
        out_shape=jax.ShapeDtypeStruct((n_tiles * tm * SUBLANES, LANES), F32),
        compiler_params=pltpu.CompilerParams(
            dimension_semantics=("arbitrary",), vmem_limit_bytes=VMEM_LIMIT),
        name="experts",
    )(tile_e, n_used, row_tok, hf3, wg, wu, wd)


def _combine_kernel(pstart_ref, e_hbm, rank_hbm, y_hbm, x2_ref, route_ref, gfin_ref, o_ref,
                    e_s0, rank_s0, e_s1, rank_s1, yg0, yg1, sem_i, sem_g, *, tm):
    def consume(ys):
        route = route_ref[...]
        w1 = route[:, 2:3]
        w2 = route[:, 3:4]
        pieces = []
        ss = jnp.zeros((tm, 1), F32)
        for s8 in range(SUBLANES):
            y1 = ys[pl.ds(s8, tm, stride=TOP_K * SUBLANES), :]
            y2 = ys[pl.ds(SUBLANES + s8, tm, stride=TOP_K * SUBLANES), :]
            xo = x2_ref[:, s8 * LANES:(s8 + 1) * LANES] + (y1 * w1 + y2 * w2)
            ss = ss + jnp.sum(xo * xo, axis=-1, keepdims=True)
            pieces.append(xo)
        d = x2_ref.shape[1]
        r = lax.rsqrt(ss * (1.0 / d) + EPS)
        for s8 in range(SUBLANES):
            o_ref[:, s8 * LANES:(s8 + 1) * LANES] = (
                pieces[s8] * r * gfin_ref[:, s8 * LANES:(s8 + 1) * LANES])

    _gather_ring(pl.program_id(0), pl.num_programs(0), (e_hbm, rank_hbm),
                 ((e_s0, rank_s0), (e_s1, rank_s1)),
                 lambda refs, r: pstart_ref[refs[0][r]] + refs[1][r],
                 y_hbm, (yg0, yg1), sem_i, sem_g, TOP_K * tm, consume)


def _combine(p_start, slot_e, slot_rank, ybuf, x2, route, g_final, tm):
    t, d = x2.shape
    row = lambda w: pl.BlockSpec((tm, w), lambda i, ps: (i, 0))
    grid_spec = pltpu.PrefetchScalarGridSpec(
        num_scalar_prefetch=1,
        grid=(t // tm,),
        in_specs=[
            pl.BlockSpec(memory_space=pl.ANY),
            pl.BlockSpec(memory_space=pl.ANY),
            pl.BlockSpec(memory_space=pl.ANY),
            row(d), row(LANES),
            pl.BlockSpec(g_final.shape, lambda i, ps: (0, 0)),
        ],
        out_specs=row(d),
        scratch_shapes=[pltpu.SMEM((TOP_K * tm,), I32) for _ in range(4)] + [
            pltpu.VMEM((TOP_K * tm * SUBLANES, LANES), F32),
            pltpu.VMEM((TOP_K * tm * SUBLANES, LANES), F32),
            pltpu.SemaphoreType.DMA((2,)),
            pltpu.SemaphoreType.DMA((2,)),
        ],
    )
    return pl.pallas_call(
        functools.partial(_combine_kernel, tm=tm),
        grid_spec=grid_spec,
        out_shape=jax.ShapeDtypeStruct((t, d), F32),
        compiler_params=pltpu.CompilerParams(
            dimension_semantics=("arbitrary",), vmem_limit_bytes=VMEM_LIMIT),
        name="combine",
    )(p_start, slot_e, slot_rank, ybuf, x2, route, g_final)


def _plan_dispatch(route, counts_row, moe_tm):
    t = route.shape[0]
    n = t * TOP_K
    n_tiles = n // moe_tm + N_EXPERTS
    e = route[:, 0:TOP_K].astype(I32)
    rank = route[:, 4:4 + TOP_K].astype(I32)
    counts = counts_row[0, :N_EXPERTS].astype(I32)
    padded = (counts + moe_tm - 1) // moe_tm * moe_tm
    p_end = jnp.cumsum(padded)
    p_start = (p_end - padded).astype(I32)
    start = jnp.cumsum(counts) - counts
    tile_start = jnp.arange(n_tiles, dtype=I32) * moe_tm
    tile_e = jnp.minimum(jnp.sum(p_end[None, :] <= tile_start[:, None], axis=1), N_EXPERTS - 1)
    tile_e = tile_e.astype(I32)
    n_used = (p_end[-1] // moe_tm).astype(I32).reshape(1)
    order = jnp.argsort(e.reshape(n), stable=True).astype(I32)
    off = jnp.arange(moe_tm, dtype=I32)[None, :] + (tile_start - p_start[tile_e])[:, None]
    valid = off < counts[tile_e][:, None]
    src = jnp.clip(start[tile_e][:, None] + off, 0, n - 1)
    row_tok = jnp.where(valid, order[src] // TOP_K, 0)
    return tile_e, n_used, row_tok, p_start, e, rank


def _lane_row(vals, offset):
    row = jnp.zeros((1, LANES), F32)
    return row.at[0, offset:offset + vals.shape[0]].set(vals.astype(F32))


def _forward(x, mem, g_mix, w_in, dn_conv_w, dn_a_log, dn_dt_bias, dn_norm_g, w_out,
             g_xattn, g_mem, xa_wq, xa_wkv, xa_wo, g_ffn, w_grp, b_grp, w_exp, b_exp,
             we_gate, we_up, we_down, g_final, *, tm_in, tb_dn, tq_sb, tm_mix, moe_tm, tm_out):
    b, s, d = x.shape
    t = b * s
    x2d = x.reshape(t, d)
    assert g_mix.shape[0] == 1, "the final RMSNorm is fused into the (single) layer's combine step"
    for l in range(1):
        w = w_in[l]
        o0 = 3 * SB_W
        o1 = o0 + 3 * DN_W
        o2 = o1 + 2 * DN_HEADS
        o3 = o2 + DN_W
        wsb = jnp.concatenate([w[:, :SB_W] * (SB_DH ** -0.5), w[:, SB_W:o0]], axis=1).astype(BF16)
        wdn = w[:, o0:o1].astype(BF16)
        wba = jnp.pad(w[:, o1:o2], ((0, 0), (0, LANES - 2 * DN_HEADS))).astype(BF16)
        wz = w[:, o2:o3].astype(BF16)
        wg = w[:, o3:].astype(BF16)
        sb, dn, ba, z, gates = _inproj(x2d, g_mix[l][None, :], wsb, wdn, wba, wz, wg,
                                       dn_conv_w[l], tm_in, s // tm_in)

        o_sb = _sbattn(sb, b, s, tq_sb)
        o_dn = _deltanet(dn, ba, z, _lane_row(dn_a_log[l], DN_HEADS),
                         _lane_row(dn_dt_bias[l], DN_HEADS), dn_norm_g[l][None, :], b, s, tb_dn)

        wkv = xa_wkv[l]
        kt, v = _memkv(mem, g_mem[l][None, :], wkv[:, :XA_W].T.astype(BF16),
                       wkv[:, XA_W:].astype(BF16))

        w_route = jnp.pad(jnp.concatenate([w_grp[l], w_exp[l]], axis=1),
                          ((0, 0), (0, LANES - N_GROUPS - N_EXPERTS)))
        wr_hi = w_route.astype(BF16)
        wr_lo = (w_route - wr_hi.astype(F32)).astype(BF16)
        b_route = _lane_row(jnp.concatenate([b_grp[l], b_exp[l]]), 0)
        x2, hf3, route, counts = _mix(x2d, o_sb, o_dn, gates, w_out[l].astype(BF16),
                                      g_xattn[l][None, :], xa_wq[l].astype(BF16), kt, v,
                                      xa_wo[l].astype(BF16), g_ffn[l][None, :], wr_hi, wr_lo,
                                      b_route, s, tm_mix)

        tile_e, n_used, row_tok, p_start, slot_e, slot_rank = _plan_dispatch(route, counts, moe_tm)
        ybuf = _experts(tile_e, n_used, row_tok, hf3, we_gate[l], we_up[l], we_down[l], moe_tm)
        per_tile = (t // tm_out, TOP_K * tm_out)
        x2d = _combine(p_start, slot_e.reshape(per_tile), slot_rank.reshape(per_tile), ybuf, x2,
                       route, g_final[None, :], tm_out)
    return x2d.reshape(b, s, d)


def kernel(x, mem, g_mix, w_in, dn_conv_w, dn_a_log, dn_dt_bias, dn_norm_g, w_out, g_xattn, g_mem,
           xa_wq, xa_wkv, xa_wo, g_ffn, w_grp, b_grp, w_exp, b_exp, we_gate, we_up, we_down, g_final):
    assert g_mix.shape[0] == 1, "single-layer block"
    return _forward(x, mem, g_mix, w_in, dn_conv_w, dn_a_log, dn_dt_bias, dn_norm_g, w_out,
                    g_xattn, g_mem, xa_wq, xa_wkv, xa_wo, g_ffn, w_grp, b_grp, w_exp, b_exp,
                    we_gate, we_up, we_down, g_final,
                    tm_in=512, tb_dn=256, tq_sb=128, tm_mix=512, moe_tm=512, tm_out=256)
```
